```python
import functools
import math
import jax
import jax.numpy as jnp
from jax import lax
import numpy as np

D_MODEL = 2048
BATCH = 4
SEQ = 2048
DEPTH = 4
DEC_BATCH = 8
DEC_SEQ = 4
PAST_LEN = 16384
PAGE_SIZE = 128

N_HEADS = 8
HEAD_DIM = 128
ATTN_WIDTH = N_HEADS * HEAD_DIM
ROT_DIM = HEAD_DIM // 4
ROPE_THETA = 500000.0
BLOCK = 256
TOP_K = 3
Q_CHUNK = 16
SSM_WIDTH = D_MODEL // 2
SSM_GROUP = 16
SSM_GROUPS = SSM_WIDTH // SSM_GROUP
SSM_STATE = 64
D_FF = 5632
IN_COLS = 3 * ATTN_WIDTH + SSM_WIDTH + 2 * D_MODEL
ALPHA = (2.0 * DEPTH) ** 0.25
BETA = (8.0 * DEPTH) ** -0.25
LN_EPS = 1e-5
NEG = -1e30

kernel_name = 'moba_s5_gated_hybrid_step'


def _layer_norm(x, g, b):
    xf = x.astype(jnp.float32)
    mu = xf.mean(-1, keepdims=True)
    var = jnp.square(xf - mu).mean(-1, keepdims=True)
    y = (xf - mu) * lax.rsqrt(var + LN_EPS) * g.astype(jnp.float32) + b.astype(jnp.float32)
    return y.astype(x.dtype)


def _swiglu(x, w_gate, w_up, w_down):
    return (jax.nn.silu(x @ w_gate) * (x @ w_up)) @ w_down


def _rope_partial(x, pos):
    half = ROT_DIM // 2
    inv = jnp.power(ROPE_THETA, -jnp.arange(half, dtype=jnp.float32) * 2.0 / ROT_DIM)
    ang = pos.astype(jnp.float32)[:, None] * inv[None, :]
    cos = jnp.cos(ang)[None, :, None, :]
    sin = jnp.sin(ang)[None, :, None, :]
    xr = x[..., :ROT_DIM].astype(jnp.float32)
    x1, x2 = xr[..., :half], xr[..., half:]
    rot = jnp.concatenate([x1 * cos - x2 * sin, x2 * cos + x1 * sin], axis=-1)
    return jnp.concatenate([rot.astype(x.dtype), x[..., ROT_DIM:]], axis=-1)


def _project(x, w_in):
    n, s, _ = x.shape
    h = x @ w_in
    cuts = [ATTN_WIDTH, 2 * ATTN_WIDTH, 3 * ATTN_WIDTH, 3 * ATTN_WIDTH + SSM_WIDTH,
            3 * ATTN_WIDTH + SSM_WIDTH + D_MODEL]
    q, k, v, u, ga, gb = jnp.split(h, cuts, axis=-1)
    heads = lambda t: t.reshape(n, s, N_HEADS, HEAD_DIM)
    return heads(q), heads(k), heads(v), u, ga, gb


def _moba_prompt(q, k, v):
    b_, s_ = q.shape[0], q.shape[1]
    scale = HEAD_DIM ** -0.5
    qh = q.transpose(0, 2, 1, 3)
    kh = k.transpose(0, 2, 1, 3)
    vh = v.transpose(0, 2, 1, 3)
    nb_pad = -(-s_ // BLOCK)
    n_full = s_ // BLOCK
    k_sel_n = min(TOP_K, nb_pad - 1)
    pad = nb_pad * BLOCK - s_
    k_blocks = jnp.pad(kh, ((0, 0), (0, 0), (0, pad), (0, 0))).reshape(b_, N_HEADS, nb_pad, BLOCK, HEAD_DIM)
    v_blocks = jnp.pad(vh, ((0, 0), (0, 0), (0, pad), (0, 0))).reshape(b_, N_HEADS, nb_pad, BLOCK, HEAD_DIM)
    if k_sel_n > 0:
        kbar = kh[:, :, :n_full * BLOCK].reshape(b_, N_HEADS, n_full, BLOCK, HEAD_DIM).astype(jnp.float32).mean(3)
    b_ix = jnp.arange(b_)[:, None, None, None]
    h_ix = jnp.arange(N_HEADS)[None, :, None, None]

    def chunk(i):
        q0 = i * Q_CHUNK
        qc = lax.dynamic_slice_in_dim(qh, q0, Q_CHUNK, axis=2)
        qpos = q0 + jnp.arange(Q_CHUNK)
        c = q0 // BLOCK
        k_own = lax.dynamic_index_in_dim(k_blocks, c, axis=2, keepdims=False)
        v_own = lax.dynamic_index_in_dim(v_blocks, c, axis=2, keepdims=False)
        kpos = c * BLOCK + jnp.arange(BLOCK)
        s_own = jnp.einsum('bhqd,bhld->bhql', qc, k_own).astype(jnp.float32) * scale
        s_own = jnp.where(kpos[None, :] <= qpos[:, None], s_own, NEG)
        if k_sel_n == 0:
            p = jax.nn.softmax(s_own, axis=-1)
            return jnp.einsum('bhql,bhld->bhqd', p.astype(v_own.dtype), v_own)
        gate = jnp.einsum('bhqd,bhnd->bhqn', qc.astype(jnp.float32), kbar)
        gate = jnp.where(jnp.arange(n_full) < c, gate, NEG)
        _, sel = lax.top_k(gate, k_sel_n)
        valid = sel < c
        k_g = k_blocks[b_ix, h_ix, sel]
        v_g = v_blocks[b_ix, h_ix, sel]
        s_sel = jnp.einsum('bhqd,bhqkld->bhqkl', qc, k_g).astype(jnp.float32) * scale
        s_sel = jnp.where(valid[..., None], s_sel, NEG).reshape(b_, N_HEADS, Q_CHUNK, k_sel_n * BLOCK)
        p = jax.nn.softmax(jnp.concatenate([s_sel, s_own], axis=-1), axis=-1)
        p_sel = p[..., :k_sel_n * BLOCK].reshape(b_, N_HEADS, Q_CHUNK, k_sel_n, BLOCK).astype(v_g.dtype)
        p_own = p[..., k_sel_n * BLOCK:].astype(v_own.dtype)
        return (jnp.einsum('bhqkl,bhqkld->bhqd', p_sel, v_g)
                + jnp.einsum('bhql,bhld->bhqd', p_own, v_own))

    out = lax.map(chunk, jnp.arange(s_ // Q_CHUNK))
    return out.transpose(1, 0, 3, 2, 4).reshape(b_, s_, ATTN_WIDTH)


def _moba_sample(q, k, v, cache_k, cache_v, page_table, layer):
    db, ds = q.shape[0], q.shape[1]
    scale = HEAD_DIM ** -0.5
    c = PAST_LEN // BLOCK
    r = PAST_LEN - c * BLOCK
    k_sel_n = min(TOP_K, c)
    if r > 0:
        pos_own = c * BLOCK + jnp.arange(r)
        phys_own = page_table[:, pos_own // PAGE_SIZE]
        off_own = pos_own % PAGE_SIZE
        k_own = jnp.concatenate([cache_k[layer, phys_own, off_own], k], axis=1)
        v_own = jnp.concatenate([cache_v[layer, phys_own, off_own], v], axis=1)
    else:
        k_own, v_own = k, v
    j = jnp.arange(r + ds)
    i = jnp.arange(ds)
    own_mask = (j[None, :] < r) | (j[None, :] - r <= i[:, None])
    s_own = jnp.einsum('bqhd,bkhd->bhqk', q, k_own).astype(jnp.float32) * scale
    s_own = jnp.where(own_mask, s_own, NEG)
    if k_sel_n == 0:
        p = jax.nn.softmax(s_own, axis=-1)
        o = jnp.einsum('bhqk,bkhd->bhqd', p.astype(v_own.dtype), v_own)
        return o.transpose(0, 2, 1, 3).reshape(db, ds, ATTN_WIDTH)
    k_past = cache_k[layer, page_table].reshape(db, PAST_LEN, N_HEADS, HEAD_DIM)[:, :c * BLOCK]
    kbar = k_past.reshape(db, c, BLOCK, N_HEADS, HEAD_DIM).astype(jnp.float32).mean(2)
    gate = jnp.einsum('bqhd,bnhd->bhqn', q.astype(jnp.float32), kbar)
    _, sel = lax.top_k(gate, k_sel_n)
    pos = sel[..., None] * BLOCK + jnp.arange(BLOCK)
    b_ix = jnp.arange(db)[:, None, None, None, None]
    h_ix = jnp.arange(N_HEADS)[None, :, None, None, None]
    phys = page_table[b_ix, pos // PAGE_SIZE]
    off = pos % PAGE_SIZE
    k_g = cache_k[layer, phys, off, h_ix]
    v_g = cache_v[layer, phys, off, h_ix]
    s_sel = jnp.einsum('bqhd,bhqkld->bhqkl', q, k_g).astype(jnp.float32) * scale
    s_sel = s_sel.reshape(db, N_HEADS, ds, k_sel_n * BLOCK)
    p = jax.nn.softmax(jnp.concatenate([s_sel, s_own], axis=-1), axis=-1)
    p_sel = p[..., :k_sel_n * BLOCK].reshape(db, N_HEADS, ds, k_sel_n, BLOCK).astype(v_g.dtype)
    p_own = p[..., k_sel_n * BLOCK:].astype(v_own.dtype)
    o = (jnp.einsum('bhqkl,bhqkld->bhqd', p_sel, v_g)
         + jnp.einsum('bhqk,bkhd->bhqd', p_own, v_own))
    return o.transpose(0, 2, 1, 3).reshape(db, ds, ATTN_WIDTH)


def _lin_combine(e1, e2):
    a1, b1 = e1
    a2, b2 = e2
    return a1 * a2, a2 * b1 + b2


def _s5(u, h0_re, h0_im, a_re, a_im, log_dt, b_re, b_im, c_re, c_im, d_skip):
    n, s_, _ = u.shape
    f32 = jnp.float32
    lam = lax.complex(a_re.astype(f32), a_im.astype(f32))
    dt = jnp.exp(log_dt.astype(f32))[:, None]
    lam_bar = jnp.exp(lam * dt)
    b_bar = ((lam_bar - 1.0) / lam)[:, :, None] * lax.complex(b_re.astype(f32), b_im.astype(f32))
    c_mat = lax.complex(c_re.astype(f32), c_im.astype(f32))
    ug = u.astype(f32).reshape(n, s_, SSM_GROUPS, SSM_GROUP)
    bu = jnp.einsum('nsgc,gpc->nsgp', ug.astype(jnp.complex64), b_bar)
    a = jnp.broadcast_to(lam_bar, bu.shape)
    a_cum, h = lax.associative_scan(_lin_combine, (a, bu), axis=1)
    h = h + a_cum * lax.complex(h0_re.astype(f32), h0_im.astype(f32))[:, None]
    y = jnp.einsum('nsgp,gcp->nsgc', h, c_mat).real + d_skip.astype(f32).reshape(SSM_GROUPS, SSM_GROUP) * ug
    h_last = h[:, -1]
    return y.reshape(n, s_, SSM_WIDTH).astype(u.dtype), jnp.real(h_last), jnp.imag(h_last)


def _layer(x, pos, attend, h0_re, h0_im, lw):
    (w_in, w_attn_out, w_glu_v, w_glu_g, w_out, a_re, a_im, log_dt, b_re, b_im, c_re, c_im, d_skip,
     f1g, f1u, f1d, f2g, f2u, f2d, g, b) = lw
    x = _layer_norm(ALPHA * x + 0.5 * _swiglu(x, f1g, f1u, f1d), g[0], b[0])
    q, k, v, u, ga, gb = _project(x, w_in)
    q = _rope_partial(q, pos)
    k = _rope_partial(k, pos)
    attn = attend(q, k, v)
    y_ssm, h_re, h_im = _s5(u, h0_re, h0_im, a_re, a_im, log_dt, b_re, b_im, c_re, c_im, d_skip)
    y_a = attn @ w_attn_out
    z = jax.nn.gelu(y_ssm)
    y_b = (z @ w_glu_v) * jax.nn.sigmoid(z @ w_glu_g)
    mix = (jax.nn.sigmoid(ga) * y_a + jax.nn.sigmoid(gb) * y_b) @ w_out
    x = _layer_norm(ALPHA * x + mix, g[1], b[1])
    x = _layer_norm(ALPHA * x + 0.5 * _swiglu(x, f2g, f2u, f2d), g[2], b[2])
    return x, k, v, h_re, h_im


def setup_inputs(seed: int = 0) -> dict:
    key = jax.random.key(seed)
    ks = jax.random.split(key, 32)
    n_pages = PAST_LEN // PAGE_SIZE
    n_pool = (DEC_BATCH * n_pages * 5) // 4
    nrm = lambda kk, shape, s: jax.random.normal(kk, shape, jnp.float32) * s
    x_prompt = nrm(ks[0], (BATCH, SEQ, D_MODEL), 1.0)
    x_sample = nrm(ks[1], (DEC_BATCH, DEC_SEQ, D_MODEL), 1.0)
    cache_k = nrm(ks[2], (DEPTH, n_pool, PAGE_SIZE, N_HEADS, HEAD_DIM), 1.0)
    cache_v = nrm(ks[3], (DEPTH, n_pool, PAGE_SIZE, N_HEADS, HEAD_DIM), 1.0)
    state_ssm_re = nrm(ks[4], (DEPTH, DEC_BATCH, SSM_GROUPS, SSM_STATE), 0.5)
    state_ssm_im = nrm(ks[5], (DEPTH, DEC_BATCH, SSM_GROUPS, SSM_STATE), 0.5)
    page_table = jax.random.permutation(ks[6], n_pool)[:DEC_BATCH * n_pages].reshape(DEC_BATCH, n_pages).astype(jnp.int32)
    w_in = nrm(ks[7], (DEPTH, D_MODEL, IN_COLS), D_MODEL ** -0.5)
    w_attn_out = nrm(ks[8], (DEPTH, ATTN_WIDTH, D_MODEL), ATTN_WIDTH ** -0.5)
    w_glu_v = nrm(ks[9], (DEPTH, SSM_WIDTH, D_MODEL), SSM_WIDTH ** -0.5)
    w_glu_g = nrm(ks[10], (DEPTH, SSM_WIDTH, D_MODEL), SSM_WIDTH ** -0.5)
    w_out = nrm(ks[11], (DEPTH, D_MODEL, D_MODEL), BETA * D_MODEL ** -0.5)
    ssm_a_re = -0.5 + nrm(ks[12], (DEPTH, SSM_GROUPS, SSM_STATE), 0.01)
    ssm_a_im = math.pi * jnp.arange(SSM_STATE, dtype=jnp.float32) + nrm(ks[13], (DEPTH, SSM_GROUPS, SSM_STATE), 0.01)
    ssm_log_dt = jax.random.uniform(ks[14], (DEPTH, SSM_GROUPS), jnp.float32, math.log(1e-3), math.log(1e-1))
    ssm_b_re = nrm(ks[15], (DEPTH, SSM_GROUPS, SSM_STATE, SSM_GROUP), (2.0 * SSM_GROUP) ** -0.5)
    ssm_b_im = nrm(ks[16], (DEPTH, SSM_GROUPS, SSM_STATE, SSM_GROUP), (2.0 * SSM_GROUP) ** -0.5)
    ssm_c_re = nrm(ks[17], (DEPTH, SSM_GROUPS, SSM_GROUP, SSM_STATE), (2.0 * SSM_STATE) ** -0.5)
    ssm_c_im = nrm(ks[18], (DEPTH, SSM_GROUPS, SSM_GROUP, SSM_STATE), (2.0 * SSM_STATE) ** -0.5)
    ssm_d = nrm(ks[19], (DEPTH, SSM_WIDTH), 1.0)
    ffn1_gate = nrm(ks[20], (DEPTH, D_MODEL, D_FF), D_MODEL ** -0.5)
    ffn1_up = nrm(ks[21], (DEPTH, D_MODEL, D_FF), D_MODEL ** -0.5)
    ffn1_down = nrm(ks[22], (DEPTH, D_FF, D_MODEL), BETA * D_FF ** -0.5)
    ffn2_gate = nrm(ks[23], (DEPTH, D_MODEL, D_FF), D_MODEL ** -0.5)
    ffn2_up = nrm(ks[24], (DEPTH, D_MODEL, D_FF), D_MODEL ** -0.5)
    ffn2_down = nrm(ks[25], (DEPTH, D_FF, D_MODEL), BETA * D_FF ** -0.5)
    ln_gain = 1.0 + nrm(ks[26], (DEPTH, 3, D_MODEL), 0.01)
    ln_bias = nrm(ks[27], (DEPTH, 3, D_MODEL), 0.01)
    return {'x_prompt': x_prompt, 'x_sample': x_sample, 'cache_k': cache_k, 'cache_v': cache_v,
            'state_ssm_re': state_ssm_re, 'state_ssm_im': state_ssm_im, 'page_table': page_table,
            'w_in': w_in, 'w_attn_out': w_attn_out, 'w_glu_v': w_glu_v, 'w_glu_g': w_glu_g, 'w_out': w_out,
            'ssm_a_re': ssm_a_re, 'ssm_a_im': ssm_a_im, 'ssm_log_dt': ssm_log_dt,
            'ssm_b_re': ssm_b_re, 'ssm_b_im': ssm_b_im, 'ssm_c_re': ssm_c_re, 'ssm_c_im': ssm_c_im, 'ssm_d': ssm_d,
            'ffn1_gate': ffn1_gate, 'ffn1_up': ffn1_up, 'ffn1_down': ffn1_down,
            'ffn2_gate': ffn2_gate, 'ffn2_up': ffn2_up, 'ffn2_down': ffn2_down,
            'ln_gain': ln_gain, 'ln_bias': ln_bias}


def reference(x_prompt, x_sample, cache_k, cache_v, state_ssm_re, state_ssm_im, page_table,
              w_in, w_attn_out, w_glu_v, w_glu_g, w_out,
              ssm_a_re, ssm_a_im, ssm_log_dt, ssm_b_re, ssm_b_im, ssm_c_re, ssm_c_im, ssm_d,
              ffn1_gate, ffn1_up, ffn1_down, ffn2_gate, ffn2_up, ffn2_down, ln_gain, ln_bias):
    pos_prompt = jnp.arange(SEQ, dtype=jnp.int32)
    pos_sample = PAST_LEN + jnp.arange(DEC_SEQ, dtype=jnp.int32)
    zero_state = jnp.zeros((BATCH, SSM_GROUPS, SSM_STATE), jnp.float32)
    yp, ys = x_prompt, x_sample
    kp_l, vp_l, hrp_l, hip_l = [], [], [], []
    ks_l, vs_l, hrs_l, his_l = [], [], [], []
    for l in range(DEPTH):
        lw = (w_in[l], w_attn_out[l], w_glu_v[l], w_glu_g[l], w_out[l],
              ssm_a_re[l], ssm_a_im[l], ssm_log_dt[l], ssm_b_re[l], ssm_b_im[l], ssm_c_re[l], ssm_c_im[l], ssm_d[l],
              ffn1_gate[l], ffn1_up[l], ffn1_down[l], ffn2_gate[l], ffn2_up[l], ffn2_down[l],
              ln_gain[l], ln_bias[l])
        yp, kp, vp, hrp, hip = _layer(yp, pos_prompt, _moba_prompt, zero_state, zero_state, lw)
        attend_s = functools.partial(_moba_sample, cache_k=cache_k, cache_v=cache_v,
                                     page_table=page_table, layer=l)
        ys, ksm, vsm, hrs, his = _layer(ys, pos_sample, attend_s, state_ssm_re[l], state_ssm_im[l], lw)
        kp_l.append(kp); vp_l.append(vp); hrp_l.append(hrp); hip_l.append(hip)
        ks_l.append(ksm); vs_l.append(vsm); hrs_l.append(hrs); his_l.append(his)
    return (yp, ys,
            jnp.stack(kp_l), jnp.stack(vp_l), jnp.stack(hrp_l), jnp.stack(hip_l),
            jnp.stack(ks_l), jnp.stack(vs_l), jnp.stack(hrs_l), jnp.stack(his_l))
```

```python
import functools
import math

import jax
import jax.numpy as jnp
from jax import lax
from jax.experimental import pallas as pl
from jax.experimental.pallas import tpu as pltpu

N_HEADS = 8
HEAD_DIM = 128
ATTN_WIDTH = N_HEADS * HEAD_DIM
ROT_DIM = HEAD_DIM // 4
ROT_HALF = ROT_DIM // 2
ROPE_THETA = 500000.0
BLOCK = 256
TOP_K = 3
PAGE_SIZE = 128
PAGES_PER_BLOCK = BLOCK // PAGE_SIZE
SSM_GROUP = 16
SSM_STATE = 64
SSM_CHUNK = 16
LN_EPS = 1e-5
NEG = -1e30

ROW_TILE = 512
VMEM_LIMIT = 56 * 1024 * 1024

F32 = jnp.float32
BF16 = jnp.bfloat16


def _params(*sem):
    return pltpu.CompilerParams(dimension_semantics=sem, vmem_limit_bytes=VMEM_LIMIT)


def _layer_norm(r, g, b):
    mu = jnp.mean(r, axis=-1, keepdims=True)
    d = r - mu
    var = jnp.mean(d * d, axis=-1, keepdims=True)
    return d * lax.rsqrt(var + LN_EPS) * g + b


def _dot(a, b):
    return jnp.dot(a, b, preferred_element_type=F32)


def _dot_nt(a, b, precision=None):
    return lax.dot_general(a, b, (((1,), (1,)), ((), ())), precision=precision,
                           preferred_element_type=F32)


def _ffn_ln_kernel(x_ref, wg_ref, wu_ref, wd_ref, g_ref, b_ref, o_ref, xb_ref, acc_ref, *, alpha):
    j = pl.program_id(1)

    @pl.when(j == 0)
    def _():
        xb_ref[...] = x_ref[...].astype(BF16)
        acc_ref[...] = jnp.zeros_like(acc_ref)

    xb = xb_ref[...]
    gate = _dot(xb, wg_ref[...])
    up = _dot(xb, wu_ref[...])
    hid = (gate * jax.nn.sigmoid(gate) * up).astype(BF16)
    acc_ref[...] += _dot(hid, wd_ref[...])

    @pl.when(j == pl.num_programs(1) - 1)
    def _():
        o_ref[...] = _layer_norm(alpha * x_ref[...] + 0.5 * acc_ref[...], g_ref[...], b_ref[...])


def _ffn_ln(x, wg, wu, wd, gain, bias, layer, alpha, tf=512):
    t, d = x.shape
    f = wg.shape[-1]
    tm = ROW_TILE
    return pl.pallas_call(
        functools.partial(_ffn_ln_kernel, alpha=alpha),
        grid=(t // tm, f // tf),
        in_specs=[
            pl.BlockSpec((tm, d), lambda i, j: (i, 0)),
            pl.BlockSpec((None, d, tf), lambda i, j: (layer, 0, j)),
            pl.BlockSpec((None, d, tf), lambda i, j: (layer, 0, j)),
            pl.BlockSpec((None, tf, d), lambda i, j: (layer, j, 0)),
            pl.BlockSpec((1, d), lambda i, j: (0, 0)),
            pl.BlockSpec((1, d), lambda i, j: (0, 0)),
        ],
        out_specs=pl.BlockSpec((tm, d), lambda i, j: (i, 0)),
        out_shape=jax.ShapeDtypeStruct((t, d), F32),
        scratch_shapes=[pltpu.VMEM((tm, d), BF16), pltpu.VMEM((tm, d), F32)],
        compiler_params=_params("parallel", "arbitrary"),
        name="ffn_ln",
    )(x, wg, wu, wd, gain, bias)


def _inproj_kernel(x_ref, w_ref, c_ref, s1_ref, s2_ref, o_ref, xb_ref, *, tn):
    j = pl.program_id(1)

    @pl.when(j == 0)
    def _():
        xb_ref[...] = x_ref[...].astype(BF16)

    y = _dot(xb_ref[...], w_ref[...])
    n_rot_tiles = 2 * ATTN_WIDTH // tn

    @pl.when(j < n_rot_tiles)
    def _():
        c = c_ref[...]
        s1 = s1_ref[...]
        s2 = s2_ref[...]
        for hh in range(tn // HEAD_DIM):
            yh = y[:, hh * HEAD_DIM:(hh + 1) * HEAD_DIM]
            o_ref[:, hh * HEAD_DIM:(hh + 1) * HEAD_DIM] = (
                yh * c
                + pltpu.roll(yh, ROT_HALF, 1) * s1
                + pltpu.roll(yh, HEAD_DIM - ROT_HALF, 1) * s2)

    @pl.when(j >= n_rot_tiles)
    def _():
        o_ref[...] = y


def _in_proj(x, w_in, rope_c, rope_s1, rope_s2, layer, tn=1024):
    t, d = x.shape
    n = w_in.shape[-1]
    tm = ROW_TILE
    return pl.pallas_call(
        functools.partial(_inproj_kernel, tn=tn),
        grid=(t // tm, n // tn),
        in_specs=[
            pl.BlockSpec((tm, d), lambda i, j: (i, 0)),
            pl.BlockSpec((None, d, tn), lambda i, j: (layer, 0, j)),
            pl.BlockSpec((tm, HEAD_DIM), lambda i, j: (i, 0)),
            pl.BlockSpec((tm, HEAD_DIM), lambda i, j: (i, 0)),
            pl.BlockSpec((tm, HEAD_DIM), lambda i, j: (i, 0)),
        ],
        out_specs=pl.BlockSpec((tm, tn), lambda i, j: (i, j)),
        out_shape=jax.ShapeDtypeStruct((t, n), F32),
        scratch_shapes=[pltpu.VMEM((tm, d), BF16)],
        compiler_params=_params("parallel", "arbitrary"),
        name="in_proj_rope",
    )(x, w_in, rope_c, rope_s1, rope_s2)


def _rope_tables(pos):
    inv = jnp.power(ROPE_THETA, -jnp.arange(ROT_HALF, dtype=F32) * 2.0 / ROT_DIM)
    ang = pos.astype(F32)[:, None] * inv[None, :]
    cos, sin = jnp.cos(ang), jnp.sin(ang)
    t = pos.shape[0]
    ones = jnp.ones((t, HEAD_DIM - ROT_DIM), F32)
    zeros_h = jnp.zeros((t, ROT_HALF), F32)
    zeros_r = jnp.zeros((t, HEAD_DIM - ROT_DIM), F32)
    c = jnp.concatenate([cos, cos, ones], axis=1)
    s1 = jnp.concatenate([zeros_h, sin, zeros_r], axis=1)
    s2 = jnp.concatenate([-sin, zeros_h, zeros_r], axis=1)
    return c, s1, s2


def _moba_prompt_kernel(q_ref, k_ref, v_ref, o_ref, kbar_ref, *, n_blocks, scale):
    c = pl.program_id(2)

    @pl.when(c == 0)
    def _():
        for n in range(n_blocks):
            kbar_ref[n:n + 1, :] = jnp.mean(k_ref[n * BLOCK:(n + 1) * BLOCK, :], axis=0, keepdims=True)

    q = q_ref[...]
    qb = q.astype(BF16)
    lane = lax.broadcasted_iota(jnp.int32, (BLOCK, HEAD_DIM), 1)

    g_cols = [jnp.sum(q * kbar_ref[n:n + 1, :], axis=-1, keepdims=True) for n in range(n_blocks)]
    gates = jnp.zeros((BLOCK, HEAD_DIM), F32)
    for n in range(n_blocks):
        gates = jnp.where(lane == n, g_cols[n], gates)
    rank = jnp.zeros((BLOCK, HEAD_DIM), F32)
    for m in range(n_blocks):
        past = jnp.where(m < c, 1.0, 0.0)
        ge = jnp.where(g_cols[m] >= gates, past, 0.0)
        gt = jnp.where(g_cols[m] > gates, past, 0.0)
        rank = rank + jnp.where(lane > m, ge, gt)
    sel = jnp.where(rank < TOP_K, jnp.where(lane < c, 1.0, 0.0), 0.0)

    def scores(n):
        start = pl.multiple_of(n * BLOCK, BLOCK)
        kb = k_ref[pl.ds(start, BLOCK), :].astype(BF16)
        vb = v_ref[pl.ds(start, BLOCK), :].astype(BF16)
        return _dot_nt(qb, kb) * scale, vb

    s, vb = scores(c)
    row = lax.broadcasted_iota(jnp.int32, (BLOCK, BLOCK), 0)
    col = lax.broadcasted_iota(jnp.int32, (BLOCK, BLOCK), 1)
    s = jnp.where(col <= row, s, NEG)
    m0 = jnp.max(s, axis=-1, keepdims=True)
    p = jnp.exp(s - m0)
    l0 = jnp.sum(p, axis=-1, keepdims=True)
    acc0 = _dot(p.astype(BF16), vb)

    def body(n, carry):
        m_i, l_i, acc = carry
        s, vb = scores(n)
        picked = jnp.sum(jnp.where(lane == n, sel, 0.0), axis=-1, keepdims=True)
        s = jnp.where(picked > 0.5, s, NEG)
        m_new = jnp.maximum(m_i, jnp.max(s, axis=-1, keepdims=True))
        a = jnp.exp(m_i - m_new)
        p = jnp.exp(s - m_new)
        l_new = a * l_i + jnp.sum(p, axis=-1, keepdims=True)
        acc_new = a * acc + _dot(p.astype(BF16), vb)
        return m_new, l_new, acc_new

    _, l_f, acc_f = lax.fori_loop(0, c, body, (m0, l0, acc0))
    o_ref[...] = (acc_f / l_f).astype(o_ref.dtype)


def _moba_prompt(h, batch, seq):
    n_blocks = seq // BLOCK
    return pl.pallas_call(
        functools.partial(_moba_prompt_kernel, n_blocks=n_blocks, scale=HEAD_DIM ** -0.5),
        grid=(batch, N_HEADS, n_blocks),
        in_specs=[
            pl.BlockSpec((BLOCK, HEAD_DIM), lambda b, hh, c: (b * n_blocks + c, hh)),
            pl.BlockSpec((seq, HEAD_DIM), lambda b, hh, c: (b, N_HEADS + hh)),
            pl.BlockSpec((seq, HEAD_DIM), lambda b, hh, c: (b, 2 * N_HEADS + hh)),
        ],
        out_specs=pl.BlockSpec((BLOCK, HEAD_DIM), lambda b, hh, c: (b * n_blocks + c, hh)),
        out_shape=jax.ShapeDtypeStruct((batch * seq, ATTN_WIDTH), BF16),
        scratch_shapes=[pltpu.VMEM((n_blocks, HEAD_DIM), F32)],
        compiler_params=_params("parallel", "parallel", "arbitrary"),
        name="moba_prompt",
    )(h, h, h)


KBAR_PAGES_PER_STEP = 16


def _kbar_kernel(pt_ref, *refs):
    del pt_ref
    pages = refs[:KBAR_PAGES_PER_STEP]
    o_ref = refs[KBAR_PAGES_PER_STEP]
    for blk in range(KBAR_PAGES_PER_STEP // PAGES_PER_BLOCK):
        tot = None
        for p in range(PAGES_PER_BLOCK):
            part = jnp.sum(pages[blk * PAGES_PER_BLOCK + p][...], axis=0, keepdims=True)
            tot = part if tot is None else tot + part
        o_ref[blk:blk + 1, :] = tot * (1.0 / BLOCK)


def _kbar_sample(cache_k4, pt_flat, layer, dec_batch, n_pages):
    steps = n_pages // KBAR_PAGES_PER_STEP
    blocks_per_step = KBAR_PAGES_PER_STEP // PAGES_PER_BLOCK
    width = cache_k4.shape[-1]

    def page_spec(p):
        return pl.BlockSpec(
            (None, None, PAGE_SIZE, width),
            lambda b, j, pt: (layer, pt[b * n_pages + j * KBAR_PAGES_PER_STEP + p], 0, 0))

    return pl.pallas_call(
        _kbar_kernel,
        grid_spec=pltpu.PrefetchScalarGridSpec(
            num_scalar_prefetch=1,
            grid=(dec_batch, steps),
            in_specs=[page_spec(p) for p in range(KBAR_PAGES_PER_STEP)],
            out_specs=pl.BlockSpec((None, blocks_per_step, width), lambda b, j, pt: (b, j, 0)),
        ),
        out_shape=jax.ShapeDtypeStruct((dec_batch, n_pages // PAGES_PER_BLOCK, width), F32),
        compiler_params=_params("parallel", "arbitrary"),
        name="kbar_sample",
    )(pt_flat, *([cache_k4] * KBAR_PAGES_PER_STEP))


def _sample_topk_kernel(q_ref, kbar_ref, sel_ref):
    n_past = kbar_ref.shape[0]
    rows = q_ref.shape[0]
    lane_g = lax.broadcasted_iota(jnp.int32, (rows, n_past), 1).astype(F32)
    lane_o = lax.broadcasted_iota(jnp.int32, (rows, HEAD_DIM), 1)
    for hh in range(N_HEADS):
        qh = q_ref[:, hh * HEAD_DIM:(hh + 1) * HEAD_DIM]
        kb = kbar_ref[:, hh * HEAD_DIM:(hh + 1) * HEAD_DIM]
        g = _dot_nt(qh, kb, precision=lax.Precision.HIGHEST)
        out = jnp.zeros((rows, HEAD_DIM), jnp.int32)
        for kk in range(TOP_K):
            best = jnp.max(g, axis=-1, keepdims=True)
            idx = jnp.min(jnp.where(g == best, lane_g, float(n_past)), axis=-1, keepdims=True)
            out = jnp.where(lane_o == kk, idx.astype(jnp.int32), out)
            g = jnp.where(lane_g == idx, -jnp.inf, g)
        sel_ref[hh] = out


def _sample_topk(q8, kbar):
    dec_batch, rows, width = q8.shape
    n_past = kbar.shape[1]
    return pl.pallas_call(
        _sample_topk_kernel,
        grid=(dec_batch,),
        in_specs=[
            pl.BlockSpec((None, rows, width), lambda b: (b, 0, 0)),
            pl.BlockSpec((None, n_past, width), lambda b: (b, 0, 0)),
        ],
        out_specs=pl.BlockSpec((None, N_HEADS, rows, HEAD_DIM), lambda b: (b, 0, 0, 0)),
        out_shape=jax.ShapeDtypeStruct((dec_batch, N_HEADS, rows, HEAD_DIM), jnp.int32),
        compiler_params=_params("parallel"),
        name="sample_topk",
    )(q8, kbar)


def _sample_attn_kernel(pt_ref, sel_ref, q_ref, kn_ref, vn_ref, *refs, dec_seq, scale):
    del pt_ref, sel_ref
    n_sel_pages = dec_seq * TOP_K * PAGES_PER_BLOCK
    k_pages = refs[:n_sel_pages]
    v_pages = refs[n_sel_pages:2 * n_sel_pages]
    o_ref = refs[2 * n_sel_pages]
    rows = q_ref.shape[0]
    qb = q_ref[...].astype(BF16)
    s_own_all = _dot_nt(qb, kn_ref[...].astype(BF16)) * scale
    vn = vn_ref[...].astype(BF16)
    col = lax.broadcasted_iota(jnp.int32, (1, rows), 1)
    o_ref[...] = jnp.zeros_like(o_ref)
    for qi in range(dec_seq):
        s_sel, v_sel = [], []
        for kk in range(TOP_K):
            base = (qi * TOP_K + kk) * PAGES_PER_BLOCK
            kb = jnp.concatenate([k_pages[base + p][...] for p in range(PAGES_PER_BLOCK)], axis=0)
            vb = jnp.concatenate([v_pages[base + p][...] for p in range(PAGES_PER_BLOCK)], axis=0)
            s_sel.append(_dot_nt(qb, kb.astype(BF16))[qi:qi + 1, :] * scale)
            v_sel.append(vb.astype(BF16))
        s_own = jnp.where(col <= qi, s_own_all[qi:qi + 1, :], NEG)
        m = jnp.max(s_own, axis=-1, keepdims=True)
        for s in s_sel:
            m = jnp.maximum(m, jnp.max(s, axis=-1, keepdims=True))
        p_own = jnp.exp(s_own - m)
        denom = jnp.sum(p_own, axis=-1, keepdims=True)
        out = _dot(jnp.broadcast_to(p_own, (rows, rows)).astype(BF16), vn)[0:1, :]
        for s, vb in zip(s_sel, v_sel):
            p = jnp.exp(s - m)
            denom = denom + jnp.sum(p, axis=-1, keepdims=True)
            out = out + _dot(jnp.broadcast_to(p, (rows, BLOCK)).astype(BF16), vb)[0:1, :]
        o_ref[qi:qi + 1, :] = out / denom


def _sample_attn(q8, k8, v8, cache_k4, cache_v4, pt_flat, sel_flat, layer, dec_seq, n_pages):
    dec_batch, rows, _ = q8.shape
    sel_per_head = dec_seq * TOP_K
    sel_per_batch = N_HEADS * sel_per_head

    def page_spec(qi, kk, p):
        def index_map(b, hh, pt, sel):
            blk = sel[b * sel_per_batch + hh * sel_per_head + qi * TOP_K + kk]
            return layer, pt[b * n_pages + blk * PAGES_PER_BLOCK + p], 0, hh
        return pl.BlockSpec((None, None, PAGE_SIZE, HEAD_DIM), index_map)

    page_specs = [page_spec(qi, kk, p) for qi in range(dec_seq) for kk in range(TOP_K)
                  for p in range(PAGES_PER_BLOCK)]
    n_sel_pages = len(page_specs)
    head_spec = pl.BlockSpec((None, rows, HEAD_DIM), lambda b, hh, pt, sel: (b, 0, hh))
    return pl.pallas_call(
        functools.partial(_sample_attn_kernel, dec_seq=dec_seq, scale=HEAD_DIM ** -0.5),
        grid_spec=pltpu.PrefetchScalarGridSpec(
            num_scalar_prefetch=2,
            grid=(dec_batch, N_HEADS),
            in_specs=[head_spec, head_spec, head_spec] + page_specs + page_specs,
            out_specs=head_spec,
        ),
        out_shape=jax.ShapeDtypeStruct(q8.shape, F32),
        compiler_params=_params("parallel", "arbitrary"),
        name="sample_attn",
    )(pt_flat, sel_flat, q8, k8, v8, *([cache_k4] * n_sel_pages), *([cache_v4] * n_sel_pages))


def _s5_tables(a_re, a_im, log_dt, b_re, b_im, c_re, c_im):
    hi = lax.Precision.HIGHEST
    L = SSM_CHUNK
    lam = lax.complex(a_re, a_im)
    dt = jnp.exp(log_dt)[..., None]
    lam_bar = jnp.exp(lam * dt)
    b_bar = ((lam_bar - 1.0) / lam)[..., None] * lax.complex(b_re, b_im)
    c_mat = lax.complex(c_re, c_im)
    pw = [jnp.ones_like(lam_bar)]
    for _ in range(L):
        pw.append(pw[-1] * lam_bar)
    pows = jnp.stack(pw, axis=2)
    z = (c_mat.transpose(0, 1, 3, 2)[:, :, :, None, :]
         * pows.transpose(0, 1, 3, 2)[:, :, :, :, None])
    dd, gg = z.shape[0], z.shape[1]
    z = z.reshape(dd, gg, SSM_STATE, (L + 1) * SSM_GROUP)
    zt = jnp.concatenate([jnp.real(z), -jnp.imag(z)], axis=2)
    bst = jnp.concatenate([jnp.real(b_bar), jnp.imag(b_bar)], axis=2)
    r0 = jnp.einsum('dgpi,dgpn->dgin', bst, zt[..., :L * SSM_GROUP], precision=hi)
    klag = r0.reshape(dd, gg, SSM_GROUP, L, SSM_GROUP).transpose(0, 1, 3, 2, 4)
    lag = jnp.arange(L)[None, :] - jnp.arange(L)[:, None]
    tt = jnp.where((lag >= 0)[None, None, :, :, None, None], klag[:, :, jnp.clip(lag, 0, L - 1)], 0.0)
    tt = tt.transpose(0, 1, 2, 4, 3, 5).reshape(dd, gg, L * SSM_GROUP, L * SSM_GROUP)
    nt = zt[..., SSM_GROUP:]
    mcx = pows[:, :, L - 1::-1][:, :, :L, :, None] * b_bar[:, :, None]
    mcx = mcx.transpose(0, 1, 2, 4, 3).reshape(dd, gg, L * SSM_GROUP, SSM_STATE)
    mt = jnp.concatenate([jnp.real(mcx), jnp.imag(mcx)], axis=-1)

    def lanes(zc):
        return (jnp.concatenate([jnp.real(zc), jnp.real(zc)], axis=-1),
                jnp.concatenate([-jnp.imag(zc), jnp.imag(zc)], axis=-1))

    steps = []
    cur = pows[:, :, L]
    for _ in range(8):
        steps.append(cur)
        cur = cur * cur
    ar, ai = lanes(jnp.stack(steps, axis=2))
    return dict(tt=tt, mt=mt, nt=nt, ar=ar, ai=ai, pows=pows)


def _s5_prompt_kernel(u_ref, tt_ref, mt_ref, nt_ref, ar_ref, ai_ref, d_ref, y_ref, hl_ref, *, n_seq, n_chunks):
    u = u_ref[...]
    ub = u.astype(BF16)
    w = _dot(ub, mt_ref[...])
    rows = n_seq * n_chunks
    kpos = lax.broadcasted_iota(jnp.int32, (rows, 2 * SSM_STATE), 0) % n_chunks
    stride, i = 1, 0
    while stride < n_chunks:
        sh = jnp.where(kpos >= stride, pltpu.roll(w, stride, 0), 0.0)
        w = w + sh * ar_ref[i:i + 1, :] + pltpu.roll(sh, SSM_STATE, 1) * ai_ref[i:i + 1, :]
        stride, i = stride * 2, i + 1
    h_in = jnp.where(kpos >= 1, pltpu.roll(w, 1, 0), 0.0)
    y = _dot(ub, tt_ref[...]) + _dot(h_in.astype(BF16), nt_ref[...]) + d_ref[...] * u
    y_ref[...] = y
    for n in range(n_seq):
        hl_ref[n:n + 1, :] = w[(n + 1) * n_chunks - 1:(n + 1) * n_chunks, :]


def _s5_prompt(u_g, tt, mt, nt, ar, ai, d_rep, layer, n_seq):
    groups, rows, width = u_g.shape
    n_chunks = rows // n_seq
    p2 = 2 * SSM_STATE
    return pl.pallas_call(
        functools.partial(_s5_prompt_kernel, n_seq=n_seq, n_chunks=n_chunks),
        grid=(groups,),
        in_specs=[
            pl.BlockSpec((None, rows, width), lambda g: (g, 0, 0)),
            pl.BlockSpec((None, None, width, width), lambda g: (layer, g, 0, 0)),
            pl.BlockSpec((None, None, width, p2), lambda g: (layer, g, 0, 0)),
            pl.BlockSpec((None, None, p2, width), lambda g: (layer, g, 0, 0)),
            pl.BlockSpec((None, None, 8, p2), lambda g: (layer, g, 0, 0)),
            pl.BlockSpec((None, None, 8, p2), lambda g: (layer, g, 0, 0)),
            pl.BlockSpec((None, None, 1, width), lambda g: (layer, g, 0, 0)),
        ],
        out_specs=[
            pl.BlockSpec((None, rows, width), lambda g: (g, 0, 0)),
            pl.BlockSpec((None, n_seq, p2), lambda g: (g, 0, 0)),
        ],
        out_shape=[
            jax.ShapeDtypeStruct((groups, rows, width), F32),
            jax.ShapeDtypeStruct((groups, n_seq, p2), F32),
        ],
        compiler_params=_params("parallel"),
        name="s5_prompt",
    )(u_g, tt, mt, nt, ar, ai, d_rep)


def _s5_sample_kernel(u_ref, h0_ref, tt_ref, mt_ref, nt_ref, ar_ref, ai_ref, d_ref, y_ref, hn_ref):
    hi = lax.Precision.HIGHEST
    groups = u_ref.shape[0]

    def body(g, carry):
        u = u_ref[g]
        h0 = h0_ref[g]
        y = (jnp.dot(u, tt_ref[g], precision=hi, preferred_element_type=F32)
             + jnp.dot(h0, nt_ref[g], precision=hi, preferred_element_type=F32)
             + d_ref[g] * u)
        w = jnp.dot(u, mt_ref[g], precision=hi, preferred_element_type=F32)
        y_ref[g] = y
        hn_ref[g] = h0 * ar_ref[g] + pltpu.roll(h0, SSM_STATE, 1) * ai_ref[g] + w
        return carry

    lax.fori_loop(0, groups, body, 0)


def _s5_sample(u_g, h0_g, tt_s, mt_s, nt_s, ar_s, ai_s, d_s):
    args = (u_g, h0_g, tt_s, mt_s, nt_s, ar_s, ai_s, d_s)
    full = lambda a: pl.BlockSpec(a.shape, lambda i, nd=a.ndim: (0,) * nd)
    return pl.pallas_call(
        _s5_sample_kernel,
        grid=(1,),
        in_specs=[full(a) for a in args],
        out_specs=[full(u_g), full(h0_g)],
        out_shape=[jax.ShapeDtypeStruct(u_g.shape, F32), jax.ShapeDtypeStruct(h0_g.shape, F32)],
        compiler_params=_params("arbitrary"),
        name="s5_sample",
    )(*args)


def _mix_ln_kernel(a_ref, ys_ref, ga_ref, gb_ref, x_ref, wa_ref, wv_ref, wg_ref, wo_ref, g_ref, b_ref,
                   o_ref, zb_ref, acc_ref, *, alpha):
    j = pl.program_id(1)

    @pl.when(j == 0)
    def _():
        zb_ref[...] = jax.nn.gelu(ys_ref[...]).astype(BF16)
        acc_ref[...] = jnp.zeros_like(acc_ref)

    zb = zb_ref[...]
    y_a = _dot(a_ref[...], wa_ref[...])
    y_b = _dot(zb, wv_ref[...]) * jax.nn.sigmoid(_dot(zb, wg_ref[...]))
    mixed = jax.nn.sigmoid(ga_ref[...]) * y_a + jax.nn.sigmoid(gb_ref[...]) * y_b
    acc_ref[...] += _dot(mixed.astype(BF16), wo_ref[...])

    @pl.when(j == pl.num_programs(1) - 1)
    def _():
        o_ref[...] = _layer_norm(alpha * x_ref[...] + acc_ref[...], g_ref[...], b_ref[...])


def _mix_ln(attn, y_ssm, h, x, wa, wv, wg, wo, gain, bias, layer, alpha, tn=512):
    t, d = x.shape
    aw = attn.shape[1]
    sw = y_ssm.shape[1]
    tm = ROW_TILE
    ga_blk = (3 * ATTN_WIDTH + sw) // tn
    gb_blk = ga_blk + d // tn
    return pl.pallas_call(
        functools.partial(_mix_ln_kernel, alpha=alpha),
        grid=(t // tm, d // tn),
        in_specs=[
            pl.BlockSpec((tm, aw), lambda i, j: (i, 0)),
            pl.BlockSpec((tm, sw), lambda i, j: (i, 0)),
            pl.BlockSpec((tm, tn), lambda i, j: (i, ga_blk + j)),
            pl.BlockSpec((tm, tn), lambda i, j: (i, gb_blk + j)),
            pl.BlockSpec((tm, d), lambda i, j: (i, 0)),
            pl.BlockSpec((None, aw, tn), lambda i, j: (layer, 0, j)),
            pl.BlockSpec((None, sw, tn), lambda i, j: (layer, 0, j)),
            pl.BlockSpec((None, sw, tn), lambda i, j: (layer, 0, j)),
            pl.BlockSpec((None, tn, d), lambda i, j: (layer, j, 0)),
            pl.BlockSpec((1, d), lambda i, j: (0, 0)),
            pl.BlockSpec((1, d), lambda i, j: (0, 0)),
        ],
        out_specs=pl.BlockSpec((tm, d), lambda i, j: (i, 0)),
        out_shape=jax.ShapeDtypeStruct((t, d), F32),
        scratch_shapes=[pltpu.VMEM((tm, sw), BF16), pltpu.VMEM((tm, d), F32)],
        compiler_params=_params("parallel", "arbitrary"),
        name="mix_ln",
    )(attn, y_ssm, h, h, x, wa, wv, wg, wo, gain, bias)


def kernel(x_prompt, x_sample, cache_k, cache_v, state_ssm_re, state_ssm_im, page_table, w_in, w_attn_out, w_glu_v, w_glu_g, w_out, ssm_a_re, ssm_a_im, ssm_log_dt, ssm_b_re, ssm_b_im, ssm_c_re, ssm_c_im, ssm_d, ffn1_gate, ffn1_up, ffn1_down, ffn2_gate, ffn2_up, ffn2_down, ln_gain, ln_bias):
    batch, seq, d_model = x_prompt.shape
    dec_batch, dec_seq, _ = x_sample.shape
    depth = w_in.shape[0]
    n_pool = cache_k.shape[1]
    n_pages = page_table.shape[1]
    past_len = n_pages * PAGE_SIZE
    ssm_width = ssm_d.shape[1]
    groups = ssm_width // SSM_GROUP
    assert past_len % BLOCK == 0 and seq % (BLOCK * 1) == 0 and seq % SSM_CHUNK == 0
    assert past_len // BLOCK >= TOP_K and dec_seq <= 8 and dec_seq <= SSM_CHUNK
    alpha = (2.0 * depth) ** 0.25

    t_p = batch * seq
    t_s = dec_batch * dec_seq
    t_pad = -(-(t_p + t_s) // ROW_TILE) * ROW_TILE
    x = jnp.concatenate([x_prompt.reshape(t_p, d_model), x_sample.reshape(t_s, d_model),
                         jnp.zeros((t_pad - t_p - t_s, d_model), F32)], axis=0)

    pos = jnp.concatenate([jnp.tile(jnp.arange(seq, dtype=jnp.int32), batch),
                           jnp.tile(past_len + jnp.arange(dec_seq, dtype=jnp.int32), dec_batch),
                           jnp.zeros((t_pad - t_p - t_s,), jnp.int32)])
    rope_c, rope_s1, rope_s2 = _rope_tables(pos)

    w_in_b, w_ao_b, w_gv_b, w_gg_b, w_out_b = (w.astype(BF16) for w in (w_in, w_attn_out, w_glu_v, w_glu_g, w_out))
    f1g, f1u, f1d, f2g, f2u, f2d = (w.astype(BF16) for w in
                                    (ffn1_gate, ffn1_up, ffn1_down, ffn2_gate, ffn2_up, ffn2_down))

    tabs = _s5_tables(ssm_a_re, ssm_a_im, ssm_log_dt, ssm_b_re, ssm_b_im, ssm_c_re, ssm_c_im)
    L, CH = SSM_CHUNK, SSM_GROUP
    tt_b, mt_b, nt_b = tabs['tt'].astype(BF16), tabs['mt'].astype(BF16), tabs['nt'].astype(BF16)
    d_g = ssm_d.reshape(depth, groups, 1, CH)
    d_rep = jnp.tile(d_g, (1, 1, 1, L))
    ws = dec_seq * CH
    tt_s = tabs['tt'][:, :, :ws, :ws]
    nt_s = tabs['nt'][:, :, :, :ws]
    mt_s = tabs['mt'][:, :, (L - dec_seq) * CH:, :]
    lam_ds = tabs['pows'][:, :, dec_seq]
    ar_s = jnp.concatenate([jnp.real(lam_ds), jnp.real(lam_ds)], axis=-1)[:, :, None, :]
    ai_s = jnp.concatenate([-jnp.imag(lam_ds), jnp.imag(lam_ds)], axis=-1)[:, :, None, :]
    d_s = jnp.tile(d_g, (1, 1, 1, dec_seq))
    h0_s = jnp.concatenate([state_ssm_re, state_ssm_im], axis=-1).transpose(0, 2, 1, 3)

    cache_k4 = cache_k.reshape(depth, n_pool, PAGE_SIZE, ATTN_WIDTH)
    cache_v4 = cache_v.reshape(depth, n_pool, PAGE_SIZE, ATTN_WIDTH)
    pt_flat = page_table.reshape(-1)
    row_pad = 8 - dec_seq

    def pad_rows(a):
        return jnp.pad(a, ((0, 0), (0, row_pad), (0, 0)))

    kp_l, vp_l, hrp_l, hip_l, ks_l, vs_l, hrs_l, his_l = ([] for _ in range(8))
    for l in range(depth):
        gains = [ln_gain[l, i][None, :] for i in range(3)]
        biases = [ln_bias[l, i][None, :] for i in range(3)]
        x = _ffn_ln(x, f1g, f1u, f1d, gains[0], biases[0], l, alpha)
        h = _in_proj(x, w_in_b, rope_c, rope_s1, rope_s2, l)

        attn_p = _moba_prompt(h, batch, seq)
        hs = h[t_p:t_p + t_s]
        q_s = hs[:, :ATTN_WIDTH].reshape(dec_batch, dec_seq, ATTN_WIDTH)
        k_s = hs[:, ATTN_WIDTH:2 * ATTN_WIDTH].reshape(dec_batch, dec_seq, ATTN_WIDTH)
        v_s = hs[:, 2 * ATTN_WIDTH:3 * ATTN_WIDTH].reshape(dec_batch, dec_seq, ATTN_WIDTH)
        q8, k8, v8 = pad_rows(q_s), pad_rows(k_s), pad_rows(v_s)
        kbar = _kbar_sample(cache_k4, pt_flat, l, dec_batch, n_pages)
        sel = _sample_topk(q8, kbar)[:, :, :dec_seq, :TOP_K]
        attn_s = _sample_attn(q8, k8, v8, cache_k4, cache_v4, pt_flat, sel.reshape(-1), l, dec_seq, n_pages)
        attn_s = attn_s[:, :dec_seq].reshape(t_s, ATTN_WIDTH)
        attn = jnp.concatenate([attn_p, attn_s.astype(BF16),
                                jnp.zeros((t_pad - t_p - t_s, ATTN_WIDTH), BF16)], axis=0)

        u0 = 3 * ATTN_WIDTH
        u_p = h[:t_p, u0:u0 + ssm_width].reshape(batch * (seq // L), L, groups, CH)
        u_g = u_p.transpose(2, 0, 1, 3).reshape(groups, batch * (seq // L), L * CH)
        y_g, hl_g = _s5_prompt(u_g, tt_b, mt_b, nt_b, tabs['ar'], tabs['ai'], d_rep, l, batch)
        y_p = y_g.reshape(groups, batch * (seq // L), L, CH).transpose(1, 2, 0, 3).reshape(t_p, ssm_width)
        us_g = hs[:, u0:u0 + ssm_width].reshape(dec_batch, dec_seq, groups, CH)
        us_g = us_g.transpose(2, 0, 1, 3).reshape(groups, dec_batch, ws)
        ys_g, hn_g = _s5_sample(us_g, h0_s[l], tt_s[l], mt_s[l], nt_s[l], ar_s[l], ai_s[l], d_s[l])
        y_s = ys_g.reshape(groups, dec_batch, dec_seq, CH).transpose(1, 2, 0, 3).reshape(t_s, ssm_width)
        y_ssm = jnp.concatenate([y_p, y_s, jnp.zeros((t_pad - t_p - t_s, ssm_width), F32)], axis=0)

        x = _mix_ln(attn, y_ssm, h, x, w_ao_b, w_gv_b, w_gg_b, w_out_b, gains[1], biases[1], l, alpha)
        x = _ffn_ln(x, f2g, f2u, f2d, gains[2], biases[2], l, alpha)

        kp_l.append(h[:t_p, ATTN_WIDTH:2 * ATTN_WIDTH].reshape(batch, seq, N_HEADS, HEAD_DIM))
        vp_l.append(h[:t_p, 2 * ATTN_WIDTH:3 * ATTN_WIDTH].reshape(batch, seq, N_HEADS, HEAD_DIM))
        hl = hl_g.transpose(1, 0, 2)
        hrp_l.append(hl[..., :SSM_STATE])
        hip_l.append(hl[..., SSM_STATE:])
        ks_l.append(k_s.reshape(dec_batch, dec_seq, N_HEADS, HEAD_DIM))
        vs_l.append(v_s.reshape(dec_batch, dec_seq, N_HEADS, HEAD_DIM))
        hn = hn_g.transpose(1, 0, 2)
        hrs_l.append(hn[..., :SSM_STATE])
        his_l.append(hn[..., SSM_STATE:])

    y_prompt = x[:t_p].reshape(batch, seq, d_model)
    y_sample = x[t_p:t_p + t_s].reshape(dec_batch, dec_seq, d_model)
    return (y_prompt, y_sample,
            jnp.stack(kp_l), jnp.stack(vp_l), jnp.stack(hrp_l), jnp.stack(hip_l),
            jnp.stack(ks_l), jnp.stack(vs_l), jnp.stack(hrs_l), jnp.stack(his_l))
```

```python
import functools

import jax
import jax.numpy as jnp
from jax import lax
from jax.experimental import pallas as pl
from jax.experimental.pallas import tpu as pltpu

N_HEADS = 8
HEAD_DIM = 128
ATTN_WIDTH = N_HEADS * HEAD_DIM
ROT_DIM = HEAD_DIM // 4
ROT_HALF = ROT_DIM // 2
ROPE_THETA = 500000.0
BLOCK = 256
TOP_K = 3
PAGE_SIZE = 128
PAGES_PER_BLOCK = BLOCK // PAGE_SIZE
SSM_GROUP = 16
SSM_STATE = 64
SSM_CHUNK = 16
LANES = 128
GROUPS_PER_LANE_BLOCK = LANES // SSM_GROUP
LN_EPS = 1e-5
NEG = -1e30

ROW_TILE = 512
VMEM_LIMIT = 56 * 1024 * 1024

F32 = jnp.float32
BF16 = jnp.bfloat16
HIGHEST = lax.Precision.HIGHEST


def _params(*sem):
    return pltpu.CompilerParams(dimension_semantics=sem, vmem_limit_bytes=VMEM_LIMIT)


def _layer_norm(r, g, b):
    mu = jnp.mean(r, axis=-1, keepdims=True)
    d = r - mu
    var = jnp.mean(d * d, axis=-1, keepdims=True)
    return d * lax.rsqrt(var + LN_EPS) * g + b


def _dot(a, b, precision=None):
    return jnp.dot(a, b, precision=precision, preferred_element_type=F32)


def _dot_nt(a, b, precision=None):
    return lax.dot_general(a, b, (((1,), (1,)), ((), ())), precision=precision,
                           preferred_element_type=F32)


def _ffn_ln_kernel(x_ref, wg_ref, wu_ref, wd_ref, g_ref, b_ref, o_ref, xb_ref, acc_ref, *, alpha):
    j = pl.program_id(1)

    @pl.when(j == 0)
    def _():
        xb_ref[...] = x_ref[...].astype(BF16)
        acc_ref[...] = jnp.zeros_like(acc_ref)

    xb = xb_ref[...]
    gate = _dot(xb, wg_ref[...])
    up = _dot(xb, wu_ref[...])
    hid = (gate * jax.nn.sigmoid(gate) * up).astype(BF16)
    acc_ref[...] += _dot(hid, wd_ref[...])

    @pl.when(j == pl.num_programs(1) - 1)
    def _():
        o_ref[...] = _layer_norm(alpha * x_ref[...] + 0.5 * acc_ref[...], g_ref[...], b_ref[...])


def _ffn_ln(x, wg, wu, wd, gain, bias, layer, alpha, tf=512):
    t, d = x.shape
    f = wg.shape[-1]
    tm = ROW_TILE
    return pl.pallas_call(
        functools.partial(_ffn_ln_kernel, alpha=alpha),
        grid=(t // tm, f // tf),
        in_specs=[
            pl.BlockSpec((tm, d), lambda i, j: (i, 0)),
            pl.BlockSpec((None, d, tf), lambda i, j: (layer, 0, j)),
            pl.BlockSpec((None, d, tf), lambda i, j: (layer, 0, j)),
            pl.BlockSpec((None, tf, d), lambda i, j: (layer, j, 0)),
            pl.BlockSpec((1, d), lambda i, j: (0, 0)),
            pl.BlockSpec((1, d), lambda i, j: (0, 0)),
        ],
        out_specs=pl.BlockSpec((tm, d), lambda i, j: (i, 0)),
        out_shape=jax.ShapeDtypeStruct((t, d), F32),
        scratch_shapes=[pltpu.VMEM((tm, d), BF16), pltpu.VMEM((tm, d), F32)],
        compiler_params=_params("parallel", "arbitrary"),
        name="ffn_ln",
    )(x, wg, wu, wd, gain, bias)


def _inproj_kernel(x_ref, w_ref, c_ref, s1_ref, s2_ref, o_ref, k3_ref, v3_ref, xb_ref):
    j = pl.program_id(1)

    @pl.when(j == 0)
    def _():
        xb_ref[...] = x_ref[...].astype(BF16)

    y = _dot(xb_ref[...], w_ref[...])

    def rope(hh):
        yh = y[:, hh * HEAD_DIM:(hh + 1) * HEAD_DIM]
        return (yh * c_ref[...]
                + pltpu.roll(yh, ROT_HALF, 1) * s1_ref[...]
                + pltpu.roll(yh, HEAD_DIM - ROT_HALF, 1) * s2_ref[...])

    @pl.when(j == 0)
    def _():
        for hh in range(N_HEADS):
            o_ref[:, hh * HEAD_DIM:(hh + 1) * HEAD_DIM] = rope(hh)

    @pl.when(j == 1)
    def _():
        for hh in range(N_HEADS):
            kh = rope(hh)
            o_ref[:, hh * HEAD_DIM:(hh + 1) * HEAD_DIM] = kh
            k3_ref[:, hh, :] = kh

    @pl.when(j == 2)
    def _():
        o_ref[...] = y
        for hh in range(N_HEADS):
            v3_ref[:, hh, :] = y[:, hh * HEAD_DIM:(hh + 1) * HEAD_DIM]

    @pl.when(j > 2)
    def _():
        o_ref[...] = y


def _in_proj(x, w_in, rope_c, rope_s1, rope_s2, layer):
    t, d = x.shape
    n = w_in.shape[-1]
    tm, tn = ROW_TILE, ATTN_WIDTH
    kv_spec = pl.BlockSpec((tm, N_HEADS, HEAD_DIM), lambda i, j: (i, 0, 0))
    kv_shape = jax.ShapeDtypeStruct((t, N_HEADS, HEAD_DIM), F32)
    return pl.pallas_call(
        _inproj_kernel,
        grid=(t // tm, n // tn),
        in_specs=[
            pl.BlockSpec((tm, d), lambda i, j: (i, 0)),
            pl.BlockSpec((None, d, tn), lambda i, j: (layer, 0, j)),
            pl.BlockSpec((tm, HEAD_DIM), lambda i, j: (i, 0)),
            pl.BlockSpec((tm, HEAD_DIM), lambda i, j: (i, 0)),
            pl.BlockSpec((tm, HEAD_DIM), lambda i, j: (i, 0)),
        ],
        out_specs=[pl.BlockSpec((tm, tn), lambda i, j: (i, j)), kv_spec, kv_spec],
        out_shape=[jax.ShapeDtypeStruct((t, n), F32), kv_shape, kv_shape],
        scratch_shapes=[pltpu.VMEM((tm, d), BF16)],
        compiler_params=_params("parallel", "arbitrary"),
        name="in_proj_rope",
    )(x, w_in, rope_c, rope_s1, rope_s2)


def _rope_tables(pos):
    inv = jnp.power(ROPE_THETA, -jnp.arange(ROT_HALF, dtype=F32) * 2.0 / ROT_DIM)
    ang = pos.astype(F32)[:, None] * inv[None, :]
    cos, sin = jnp.cos(ang), jnp.sin(ang)
    t = pos.shape[0]
    ones = jnp.ones((t, HEAD_DIM - ROT_DIM), F32)
    zeros_h = jnp.zeros((t, ROT_HALF), F32)
    zeros_r = jnp.zeros((t, HEAD_DIM - ROT_DIM), F32)
    c = jnp.concatenate([cos, cos, ones], axis=1)
    s1 = jnp.concatenate([zeros_h, sin, zeros_r], axis=1)
    s2 = jnp.concatenate([-sin, zeros_h, zeros_r], axis=1)
    return c, s1, s2


def _moba_prompt_kernel(q_ref, k_ref, v_ref, base_ref, o_ref, kb_ref, vt_ref, kbar_ref, *, n_blocks, scale):
    del base_ref
    c = pl.program_id(2)

    @pl.when(c == 0)
    def _():
        for n in range(n_blocks):
            kbar_ref[n:n + 1, :] = jnp.mean(k_ref[n * BLOCK:(n + 1) * BLOCK, :], axis=0, keepdims=True)
        kb_ref[...] = k_ref[...].astype(BF16)
        vt_ref[...] = v_ref[...].T.astype(BF16)

    q = q_ref[...]
    gates = _dot_nt(kbar_ref[...], q, precision=HIGHEST)
    blk_id = lax.broadcasted_iota(jnp.int32, (n_blocks, BLOCK), 0)
    rank = jnp.zeros((n_blocks, BLOCK), F32)
    for m in range(n_blocks):
        gm = gates[m:m + 1, :]
        past = jnp.where(m < c, 1.0, 0.0)
        ge = jnp.where(gm >= gates, past, 0.0)
        gt = jnp.where(gm > gates, past, 0.0)
        rank = rank + jnp.where(blk_id > m, ge, gt)
    sel = jnp.where(rank < TOP_K, jnp.where(blk_id < c, 1.0, 0.0), 0.0)

    qb = (q * scale).astype(BF16)
    key_row = lax.broadcasted_iota(jnp.int32, (BLOCK, BLOCK), 0)
    qry_col = lax.broadcasted_iota(jnp.int32, (BLOCK, BLOCK), 1)
    causal = key_row <= qry_col

    def attend(nb, first_own):
        s_t = _dot_nt(kb_ref[0:nb * BLOCK, :], qb)
        tiles, keeps = [], []
        for n in range(nb):
            keep = sel[n:n + 1, :]
            if n >= first_own:
                keep = keep + jnp.where(causal, jnp.where(c == n, 1.0, 0.0), 0.0)
            keeps.append(keep > 0.5)
            tiles.append(s_t[n * BLOCK:(n + 1) * BLOCK, :])
        m = jnp.where(keeps[0], tiles[0], NEG).max(axis=0, keepdims=True)
        for keep, tile in zip(keeps[1:], tiles[1:]):
            m = jnp.maximum(m, jnp.where(keep, tile, NEG).max(axis=0, keepdims=True))
        probs = [jnp.exp(jnp.where(keep, tile - m, NEG)) for keep, tile in zip(keeps, tiles)]
        denom = probs[0].sum(axis=0, keepdims=True)
        for p in probs[1:]:
            denom = denom + p.sum(axis=0, keepdims=True)
        p_t = jnp.concatenate([p.astype(BF16) for p in probs], axis=0)
        o_t = _dot(vt_ref[:, 0:nb * BLOCK], p_t) * (1.0 / denom)
        o_ref[...] = o_t.T.astype(o_ref.dtype)

    step = 2
    for nb in range(step, n_blocks + 1, step):
        pl.when(jnp.logical_and(c < nb, c >= nb - step))(functools.partial(attend, nb, nb - step))


def _moba_prompt(h, base, batch, seq):
    n_blocks = seq // BLOCK
    return pl.pallas_call(
        functools.partial(_moba_prompt_kernel, n_blocks=n_blocks, scale=HEAD_DIM ** -0.5),
        grid=(batch, N_HEADS, n_blocks),
        in_specs=[
            pl.BlockSpec((BLOCK, HEAD_DIM), lambda b, hh, c: (b * n_blocks + c, hh)),
            pl.BlockSpec((seq, HEAD_DIM), lambda b, hh, c: (b, N_HEADS + hh)),
            pl.BlockSpec((seq, HEAD_DIM), lambda b, hh, c: (b, 2 * N_HEADS + hh)),
            pl.BlockSpec(memory_space=pl.ANY),
        ],
        out_specs=pl.BlockSpec((BLOCK, HEAD_DIM), lambda b, hh, c: (b * n_blocks + c, hh)),
        out_shape=jax.ShapeDtypeStruct(base.shape, base.dtype),
        input_output_aliases={3: 0},
        scratch_shapes=[pltpu.VMEM((seq, HEAD_DIM), BF16), pltpu.VMEM((HEAD_DIM, seq), BF16),
                        pltpu.VMEM((n_blocks, HEAD_DIM), F32)],
        compiler_params=_params("parallel", "parallel", "arbitrary"),
        name="moba_prompt",
    )(h, h, h, base)


KBAR_PAGES_PER_STEP = 16


def _kbar_kernel(pt_ref, *refs):
    del pt_ref
    pages = refs[:KBAR_PAGES_PER_STEP]
    o_ref = refs[KBAR_PAGES_PER_STEP]
    for blk in range(KBAR_PAGES_PER_STEP // PAGES_PER_BLOCK):
        tot = None
        for p in range(PAGES_PER_BLOCK):
            part = jnp.sum(pages[blk * PAGES_PER_BLOCK + p][...], axis=0)
            tot = part if tot is None else tot + part
        o_ref[blk] = tot * (1.0 / BLOCK)


def _kbar_sample(cache_k, pt_flat, layer, dec_batch, n_pages):
    steps = n_pages // KBAR_PAGES_PER_STEP
    blocks_per_step = KBAR_PAGES_PER_STEP // PAGES_PER_BLOCK

    def page_spec(p):
        return pl.BlockSpec(
            (None, None, PAGE_SIZE, N_HEADS, HEAD_DIM),
            lambda b, j, pt: (layer, pt[b * n_pages + j * KBAR_PAGES_PER_STEP + p], 0, 0, 0))

    return pl.pallas_call(
        _kbar_kernel,
        grid_spec=pltpu.PrefetchScalarGridSpec(
            num_scalar_prefetch=1,
            grid=(dec_batch, steps),
            in_specs=[page_spec(p) for p in range(KBAR_PAGES_PER_STEP)],
            out_specs=pl.BlockSpec((None, blocks_per_step, N_HEADS, HEAD_DIM), lambda b, j, pt: (b, j, 0, 0)),
        ),
        out_shape=jax.ShapeDtypeStruct((dec_batch, n_pages // PAGES_PER_BLOCK, N_HEADS, HEAD_DIM), F32),
        compiler_params=_params("parallel", "arbitrary"),
        name="kbar_sample",
    )(pt_flat, *([cache_k] * KBAR_PAGES_PER_STEP))


def _sample_topk_kernel(q_ref, kbar_ref, sel_ref):
    n_past = kbar_ref.shape[0]
    rows = q_ref.shape[0]
    lane_g = lax.broadcasted_iota(jnp.int32, (rows, n_past), 1).astype(F32)
    lane_o = lax.broadcasted_iota(jnp.int32, (rows, HEAD_DIM), 1)
    for hh in range(N_HEADS):
        qh = q_ref[:, hh * HEAD_DIM:(hh + 1) * HEAD_DIM]
        g = _dot_nt(qh, kbar_ref[:, hh, :], precision=HIGHEST)
        out = jnp.zeros((rows, HEAD_DIM), jnp.int32)
        for kk in range(TOP_K):
            best = jnp.max(g, axis=-1, keepdims=True)
            idx = jnp.min(jnp.where(g == best, lane_g, float(n_past)), axis=-1, keepdims=True)
            out = jnp.where(lane_o == kk, idx.astype(jnp.int32), out)
            g = jnp.where(lane_g == idx, -jnp.inf, g)
        sel_ref[hh] = out


def _sample_topk(q8, kbar):
    dec_batch, rows, width = q8.shape
    n_past = kbar.shape[1]
    return pl.pallas_call(
        _sample_topk_kernel,
        grid=(dec_batch,),
        in_specs=[
            pl.BlockSpec((None, rows, width), lambda b: (b, 0, 0)),
            pl.BlockSpec((None, n_past, N_HEADS, HEAD_DIM), lambda b: (b, 0, 0, 0)),
        ],
        out_specs=pl.BlockSpec((None, N_HEADS, rows, HEAD_DIM), lambda b: (b, 0, 0, 0)),
        out_shape=jax.ShapeDtypeStruct((dec_batch, N_HEADS, rows, HEAD_DIM), jnp.int32),
        compiler_params=_params("parallel"),
        name="sample_topk",
    )(q8, kbar)


def _sample_attn_kernel(pt_ref, sel_ref, q_ref, kn_ref, vn_ref, ck_hbm, cv_hbm, o_ref, kbuf, vbuf, sem,
                        *, layer, dec_seq, n_pages, scale):
    b = pl.program_id(0)
    hd = pl.program_id(1)
    step = b * N_HEADS + hd
    n_steps = pl.num_programs(0) * N_HEADS
    slot = step % 2
    sel_per_head = dec_seq * TOP_K
    n_blk = dec_seq * TOP_K

    def page_copies(bb, hh, sl):
        copies = []
        for i in range(n_blk):
            blk = sel_ref[(bb * N_HEADS + hh) * sel_per_head + i]
            for p in range(PAGES_PER_BLOCK):
                page = pt_ref[bb * n_pages + blk * PAGES_PER_BLOCK + p]
                dst = i * PAGES_PER_BLOCK + p
                copies.append(pltpu.make_async_copy(ck_hbm.at[layer, page, :, hh, :], kbuf.at[sl, dst], sem.at[0, sl]))
                copies.append(pltpu.make_async_copy(cv_hbm.at[layer, page, :, hh, :], vbuf.at[sl, dst], sem.at[1, sl]))
        return copies

    @pl.when(step == 0)
    def _():
        for cp in page_copies(b, hd, slot):
            cp.start()

    @pl.when(step + 1 < n_steps)
    def _():
        nxt = step + 1
        for cp in page_copies(nxt // N_HEADS, nxt % N_HEADS, 1 - slot):
            cp.start()

    for cp in page_copies(b, hd, slot):
        cp.wait()

    rows = q_ref.shape[0]
    qb = q_ref[...].astype(BF16)
    s_own_all = _dot_nt(qb, kn_ref[...].astype(BF16)) * scale
    vn = vn_ref[...].astype(BF16)
    col = lax.broadcasted_iota(jnp.int32, (1, rows), 1)
    o_ref[...] = jnp.zeros_like(o_ref)
    for qi in range(dec_seq):
        s_sel, v_sel = [], []
        for kk in range(TOP_K):
            base = (qi * TOP_K + kk) * PAGES_PER_BLOCK
            kb = jnp.concatenate([kbuf[slot, base + p] for p in range(PAGES_PER_BLOCK)], axis=0)
            vb = jnp.concatenate([vbuf[slot, base + p] for p in range(PAGES_PER_BLOCK)], axis=0)
            s_sel.append(_dot_nt(qb, kb.astype(BF16))[qi:qi + 1, :] * scale)
            v_sel.append(vb.astype(BF16))
        s_own = jnp.where(col <= qi, s_own_all[qi:qi + 1, :], NEG)
        m = jnp.max(s_own, axis=-1, keepdims=True)
        for s in s_sel:
            m = jnp.maximum(m, jnp.max(s, axis=-1, keepdims=True))
        p_own = jnp.exp(s_own - m)
        denom = jnp.sum(p_own, axis=-1, keepdims=True)
        out = _dot(jnp.broadcast_to(p_own, (rows, rows)).astype(BF16), vn)[0:1, :]
        for s, vb in zip(s_sel, v_sel):
            p = jnp.exp(s - m)
            denom = denom + jnp.sum(p, axis=-1, keepdims=True)
            out = out + _dot(jnp.broadcast_to(p, (rows, BLOCK)).astype(BF16), vb)[0:1, :]
        o_ref[qi:qi + 1, :] = out / denom


def _sample_attn(q8, k8, v8, cache_k, cache_v, pt_flat, sel_flat, layer, dec_seq, n_pages):
    dec_batch, rows, _ = q8.shape
    n_sel_pages = dec_seq * TOP_K * PAGES_PER_BLOCK
    head_spec = pl.BlockSpec((None, rows, HEAD_DIM), lambda b, hh, pt, sel: (b, 0, hh))
    hbm = pl.BlockSpec(memory_space=pl.ANY)
    return pl.pallas_call(
        functools.partial(_sample_attn_kernel, layer=layer, dec_seq=dec_seq, n_pages=n_pages,
                          scale=HEAD_DIM ** -0.5),
        grid_spec=pltpu.PrefetchScalarGridSpec(
            num_scalar_prefetch=2,
            grid=(dec_batch, N_HEADS),
            in_specs=[head_spec, head_spec, head_spec, hbm, hbm],
            out_specs=head_spec,
            scratch_shapes=[pltpu.VMEM((2, n_sel_pages, PAGE_SIZE, HEAD_DIM), F32),
                            pltpu.VMEM((2, n_sel_pages, PAGE_SIZE, HEAD_DIM), F32),
                            pltpu.SemaphoreType.DMA((2, 2))],
        ),
        out_shape=jax.ShapeDtypeStruct(q8.shape, F32),
        compiler_params=_params("arbitrary", "arbitrary"),
        name="sample_attn",
    )(pt_flat, sel_flat, q8, k8, v8, cache_k, cache_v)


def _s5_tables(a_re, a_im, log_dt, b_re, b_im, c_re, c_im):
    L = SSM_CHUNK
    lam = lax.complex(a_re, a_im)
    dt = jnp.exp(log_dt)[..., None]
    lam_bar = jnp.exp(lam * dt)
    b_bar = ((lam_bar - 1.0) / lam)[..., None] * lax.complex(b_re, b_im)
    c_mat = lax.complex(c_re, c_im)
    pw = [jnp.ones_like(lam_bar)]
    for _ in range(L):
        pw.append(pw[-1] * lam_bar)
    pows = jnp.stack(pw, axis=2)
    z = (c_mat.transpose(0, 1, 3, 2)[:, :, :, None, :]
         * pows.transpose(0, 1, 3, 2)[:, :, :, :, None])
    dd, gg = z.shape[0], z.shape[1]
    z = z.reshape(dd, gg, SSM_STATE, (L + 1) * SSM_GROUP)
    zt = jnp.concatenate([jnp.real(z), -jnp.imag(z)], axis=2)
    bst = jnp.concatenate([jnp.real(b_bar), jnp.imag(b_bar)], axis=2)
    r0 = jnp.einsum('dgpi,dgpn->dgin', bst, zt[..., :L * SSM_GROUP], precision=HIGHEST)
    klag = r0.reshape(dd, gg, SSM_GROUP, L, SSM_GROUP).transpose(0, 1, 3, 2, 4)
    lag = jnp.arange(L)[None, :] - jnp.arange(L)[:, None]
    tt = jnp.where((lag >= 0)[None, None, :, :, None, None], klag[:, :, jnp.clip(lag, 0, L - 1)], 0.0)
    tt = tt.transpose(0, 1, 2, 4, 3, 5).reshape(dd, gg, L * SSM_GROUP, L * SSM_GROUP)
    nt = zt[..., SSM_GROUP:]
    mcx = pows[:, :, L - 1::-1][:, :, :, :, None] * b_bar[:, :, None]
    mcx = mcx.transpose(0, 1, 2, 4, 3).reshape(dd, gg, L * SSM_GROUP, SSM_STATE)
    mt = jnp.concatenate([jnp.real(mcx), jnp.imag(mcx)], axis=-1)

    def lanes(zc):
        return (jnp.concatenate([jnp.real(zc), jnp.real(zc)], axis=-1),
                jnp.concatenate([-jnp.imag(zc), jnp.imag(zc)], axis=-1))

    steps = []
    cur = pows[:, :, L]
    for _ in range(8):
        steps.append(cur)
        cur = cur * cur
    ar, ai = lanes(jnp.stack(steps, axis=2))
    return dict(tt=tt, mt=mt, nt=nt, ar=ar, ai=ai, pows=pows)


def _chunk_permutation():
    n = SSM_CHUNK * LANES
    src = jnp.arange(n)
    t, g, ch = src // LANES, (src % LANES) // SSM_GROUP, src % SSM_GROUP
    dst = g * (SSM_CHUNK * SSM_GROUP) + t * SSM_GROUP + ch
    return (dst[:, None] == jnp.arange(n)[None, :]).astype(BF16)


def _s5_prompt_kernel(x_ref, perm_ref, tt_ref, mt_ref, nt_ref, ar_ref, ai_ref, d_ref, base_ref, y_ref, hl_ref,
                      *, n_seq, n_chunks):
    del base_ref
    L = SSM_CHUNK
    gw = L * SSM_GROUP
    rows = n_seq * n_chunks
    u_tok = jnp.concatenate([x_ref[:, t, :].astype(BF16) for t in range(L)], axis=1)
    u_grp = _dot(u_tok, perm_ref[...]).astype(BF16)
    kpos = lax.broadcasted_iota(jnp.int32, (rows, 2 * SSM_STATE), 0) % n_chunks
    y_parts = []
    for g in range(GROUPS_PER_LANE_BLOCK):
        ub = u_grp[:, g * gw:(g + 1) * gw]
        w = _dot(ub, mt_ref[g])
        stride, i = 1, 0
        while stride < n_chunks:
            sh = jnp.where(kpos >= stride, pltpu.roll(w, stride, 0), 0.0)
            w = w + sh * ar_ref[g, i:i + 1, :] + pltpu.roll(sh, SSM_STATE, 1) * ai_ref[g, i:i + 1, :]
            stride, i = stride * 2, i + 1
        h_in = jnp.where(kpos >= 1, pltpu.roll(w, 1, 0), 0.0)
        y_parts.append((_dot(ub, tt_ref[g]) + _dot(h_in.astype(BF16), nt_ref[g])).astype(BF16))
        for n in range(n_seq):
            hl_ref[g, n:n + 1, :] = w[(n + 1) * n_chunks - 1:(n + 1) * n_chunks, :]
    y_tok = _dot_nt(jnp.concatenate(y_parts, axis=1), perm_ref[...])
    for t in range(L):
        y_ref[:, t, :] = y_tok[:, t * LANES:(t + 1) * LANES] + d_ref[...] * x_ref[:, t, :]


def _s5_prompt(h3, base3, perm, tt, mt, nt, ar, ai, d_lane, layer, n_seq_total, n_chunks, u_lane_block0,
               seq_per_step=2):
    groups = tt.shape[1]
    lane_blocks = groups // GROUPS_PER_LANE_BLOCK
    rows = seq_per_step * n_chunks
    p2 = 2 * SSM_STATE
    gw = SSM_CHUNK * SSM_GROUP
    gpb = GROUPS_PER_LANE_BLOCK
    n_perm = SSM_CHUNK * LANES
    return pl.pallas_call(
        functools.partial(_s5_prompt_kernel, n_seq=seq_per_step, n_chunks=n_chunks),
        grid=(n_seq_total // seq_per_step, lane_blocks),
        in_specs=[
            pl.BlockSpec((rows, SSM_CHUNK, LANES), lambda s, g: (s, 0, u_lane_block0 + g)),
            pl.BlockSpec((n_perm, n_perm), lambda s, g: (0, 0)),
            pl.BlockSpec((None, gpb, gw, gw), lambda s, g: (layer, g, 0, 0)),
            pl.BlockSpec((None, gpb, gw, p2), lambda s, g: (layer, g, 0, 0)),
            pl.BlockSpec((None, gpb, p2, gw), lambda s, g: (layer, g, 0, 0)),
            pl.BlockSpec((None, gpb, 8, p2), lambda s, g: (layer, g, 0, 0)),
            pl.BlockSpec((None, gpb, 8, p2), lambda s, g: (layer, g, 0, 0)),
            pl.BlockSpec((None, None, 1, LANES), lambda s, g: (layer, g, 0, 0)),
            pl.BlockSpec(memory_space=pl.ANY),
        ],
        out_specs=[
            pl.BlockSpec((rows, SSM_CHUNK, LANES), lambda s, g: (s, 0, g)),
            pl.BlockSpec((None, gpb, seq_per_step, p2), lambda s, g: (s, g, 0, 0)),
        ],
        out_shape=[
            jax.ShapeDtypeStruct(base3.shape, base3.dtype),
            jax.ShapeDtypeStruct((n_seq_total // seq_per_step, groups, seq_per_step, p2), F32),
        ],
        input_output_aliases={8: 0},
        compiler_params=_params("parallel", "arbitrary"),
        name="s5_prompt",
    )(h3, perm, tt, mt, nt, ar, ai, d_lane, base3)


def _s5_sample_kernel(u_ref, h0_ref, tt_ref, mt_ref, nt_ref, ar_ref, ai_ref, d_ref, y_ref, hn_ref):
    groups = u_ref.shape[0]

    def body(g, carry):
        u = u_ref[g]
        h0 = h0_ref[g]
        y_ref[g] = _dot(u, tt_ref[g], HIGHEST) + _dot(h0, nt_ref[g], HIGHEST) + d_ref[g] * u
        hn_ref[g] = (h0 * ar_ref[g] + pltpu.roll(h0, SSM_STATE, 1) * ai_ref[g]
                     + _dot(u, mt_ref[g], HIGHEST))
        return carry

    lax.fori_loop(0, groups, body, 0)


def _s5_sample(u_g, h0_g, tt_s, mt_s, nt_s, ar_s, ai_s, d_s):
    args = (u_g, h0_g, tt_s, mt_s, nt_s, ar_s, ai_s, d_s)
    full = lambda a: pl.BlockSpec(a.shape, lambda i, nd=a.ndim: (0,) * nd)
    return pl.pallas_call(
        _s5_sample_kernel,
        grid=(1,),
        in_specs=[full(a) for a in args],
        out_specs=[full(u_g), full(h0_g)],
        out_shape=[jax.ShapeDtypeStruct(u_g.shape, F32), jax.ShapeDtypeStruct(h0_g.shape, F32)],
        compiler_params=_params("arbitrary"),
        name="s5_sample",
    )(*args)


def _mix_ln_kernel(a_ref, ys_ref, ga_ref, gb_ref, x_ref, wa_ref, wv_ref, wg_ref, wo_ref, g_ref, b_ref,
                   o_ref, zb_ref, acc_ref, *, alpha):
    j = pl.program_id(1)

    @pl.when(j == 0)
    def _():
        zb_ref[...] = jax.nn.gelu(ys_ref[...]).astype(BF16)
        acc_ref[...] = jnp.zeros_like(acc_ref)

    zb = zb_ref[...]
    y_a = _dot(a_ref[...], wa_ref[...])
    y_b = _dot(zb, wv_ref[...]) * jax.nn.sigmoid(_dot(zb, wg_ref[...]))
    mixed = jax.nn.sigmoid(ga_ref[...]) * y_a + jax.nn.sigmoid(gb_ref[...]) * y_b
    acc_ref[...] += _dot(mixed.astype(BF16), wo_ref[...])

    @pl.when(j == pl.num_programs(1) - 1)
    def _():
        o_ref[...] = _layer_norm(alpha * x_ref[...] + acc_ref[...], g_ref[...], b_ref[...])


def _mix_ln(attn, y_ssm, h, x, wa, wv, wg, wo, gain, bias, layer, alpha, tn=512):
    t, d = x.shape
    aw = attn.shape[1]
    sw = y_ssm.shape[1]
    tm = ROW_TILE
    ga_blk = (3 * ATTN_WIDTH + sw) // tn
    gb_blk = ga_blk + d // tn
    return pl.pallas_call(
        functools.partial(_mix_ln_kernel, alpha=alpha),
        grid=(t // tm, d // tn),
        in_specs=[
            pl.BlockSpec((tm, aw), lambda i, j: (i, 0)),
            pl.BlockSpec((tm, sw), lambda i, j: (i, 0)),
            pl.BlockSpec((tm, tn), lambda i, j: (i, ga_blk + j)),
            pl.BlockSpec((tm, tn), lambda i, j: (i, gb_blk + j)),
            pl.BlockSpec((tm, d), lambda i, j: (i, 0)),
            pl.BlockSpec((None, aw, tn), lambda i, j: (layer, 0, j)),
            pl.BlockSpec((None, sw, tn), lambda i, j: (layer, 0, j)),
            pl.BlockSpec((None, sw, tn), lambda i, j: (layer, 0, j)),
            pl.BlockSpec((None, tn, d), lambda i, j: (layer, j, 0)),
            pl.BlockSpec((1, d), lambda i, j: (0, 0)),
            pl.BlockSpec((1, d), lambda i, j: (0, 0)),
        ],
        out_specs=pl.BlockSpec((tm, d), lambda i, j: (i, 0)),
        out_shape=jax.ShapeDtypeStruct((t, d), F32),
        scratch_shapes=[pltpu.VMEM((tm, sw), BF16), pltpu.VMEM((tm, d), F32)],
        compiler_params=_params("parallel", "arbitrary"),
        name="mix_ln",
    )(attn, y_ssm, h, h, x, wa, wv, wg, wo, gain, bias)


def kernel(x_prompt, x_sample, cache_k, cache_v, state_ssm_re, state_ssm_im, page_table, w_in, w_attn_out, w_glu_v, w_glu_g, w_out, ssm_a_re, ssm_a_im, ssm_log_dt, ssm_b_re, ssm_b_im, ssm_c_re, ssm_c_im, ssm_d, ffn1_gate, ffn1_up, ffn1_down, ffn2_gate, ffn2_up, ffn2_down, ln_gain, ln_bias):
    batch, seq, d_model = x_prompt.shape
    dec_batch, dec_seq, _ = x_sample.shape
    depth = w_in.shape[0]
    n_pages = page_table.shape[1]
    past_len = n_pages * PAGE_SIZE
    ssm_width = ssm_d.shape[1]
    groups = ssm_width // SSM_GROUP
    L, CH = SSM_CHUNK, SSM_GROUP
    n_chunks = seq // L
    t_p = batch * seq
    t_s = dec_batch * dec_seq
    t_pad = -(-(t_p + t_s) // ROW_TILE) * ROW_TILE
    t_tail = t_pad - t_p
    assert past_len % BLOCK == 0 and past_len // BLOCK >= TOP_K
    assert seq % (2 * BLOCK) == 0 and t_p % ROW_TILE == 0 and batch % 2 == 0
    assert dec_seq <= 8 and dec_seq <= L and n_pages % KBAR_PAGES_PER_STEP == 0
    assert (3 * ATTN_WIDTH) % LANES == 0 and groups % GROUPS_PER_LANE_BLOCK == 0
    alpha = (2.0 * depth) ** 0.25

    x = jnp.concatenate([x_prompt.reshape(t_p, d_model), x_sample.reshape(t_s, d_model),
                         jnp.zeros((t_tail - t_s, d_model), F32)], axis=0)

    pos = jnp.concatenate([jnp.tile(jnp.arange(seq, dtype=jnp.int32), batch),
                           jnp.tile(past_len + jnp.arange(dec_seq, dtype=jnp.int32), dec_batch),
                           jnp.zeros((t_tail - t_s,), jnp.int32)])
    rope_c, rope_s1, rope_s2 = _rope_tables(pos)

    w_in_b, w_ao_b, w_gv_b, w_gg_b, w_out_b = (w.astype(BF16) for w in (w_in, w_attn_out, w_glu_v, w_glu_g, w_out))
    f1g, f1u, f1d, f2g, f2u, f2d = (w.astype(BF16) for w in
                                    (ffn1_gate, ffn1_up, ffn1_down, ffn2_gate, ffn2_up, ffn2_down))

    tabs = _s5_tables(ssm_a_re, ssm_a_im, ssm_log_dt, ssm_b_re, ssm_b_im, ssm_c_re, ssm_c_im)
    tt_b, mt_b, nt_b = tabs['tt'].astype(BF16), tabs['mt'].astype(BF16), tabs['nt'].astype(BF16)
    perm = _chunk_permutation()
    d_lane = ssm_d.reshape(depth, ssm_width // LANES, 1, LANES)
    d_g = ssm_d.reshape(depth, groups, 1, CH)
    ws = dec_seq * CH
    tt_s = tabs['tt'][:, :, :ws, :ws]
    nt_s = tabs['nt'][:, :, :, :ws]
    mt_s = tabs['mt'][:, :, (L - dec_seq) * CH:, :]
    lam_ds = tabs['pows'][:, :, dec_seq]
    ar_s = jnp.concatenate([jnp.real(lam_ds), jnp.real(lam_ds)], axis=-1)[:, :, None, :]
    ai_s = jnp.concatenate([-jnp.imag(lam_ds), jnp.imag(lam_ds)], axis=-1)[:, :, None, :]
    d_s = jnp.tile(d_g, (1, 1, 1, dec_seq))
    h0_s = jnp.concatenate([state_ssm_re, state_ssm_im], axis=-1).transpose(0, 2, 1, 3)

    pt_flat = page_table.reshape(-1)
    row_pad = 8 - dec_seq

    def pad_rows(a):
        return jnp.pad(a, ((0, 0), (0, row_pad), (0, 0)))

    def tail_base(tail_rows):
        width, dtype = tail_rows.shape[1], tail_rows.dtype
        return jnp.concatenate([jnp.zeros((t_p, width), dtype), tail_rows,
                                jnp.zeros((t_tail - t_s, width), dtype)], axis=0)

    u0 = 3 * ATTN_WIDTH
    kp_l, vp_l, hrp_l, hip_l, ks_l, vs_l, hrs_l, his_l = ([] for _ in range(8))
    for l in range(depth):
        gains = [ln_gain[l, i][None, :] for i in range(3)]
        biases = [ln_bias[l, i][None, :] for i in range(3)]
        x = _ffn_ln(x, f1g, f1u, f1d, gains[0], biases[0], l, alpha)
        h, k3, v3 = _in_proj(x, w_in_b, rope_c, rope_s1, rope_s2, l)

        hs = h[t_p:t_p + t_s]
        q_s = hs[:, :ATTN_WIDTH].reshape(dec_batch, dec_seq, ATTN_WIDTH)
        k_s = hs[:, ATTN_WIDTH:2 * ATTN_WIDTH].reshape(dec_batch, dec_seq, ATTN_WIDTH)
        v_s = hs[:, 2 * ATTN_WIDTH:3 * ATTN_WIDTH].reshape(dec_batch, dec_seq, ATTN_WIDTH)
        q8, k8, v8 = pad_rows(q_s), pad_rows(k_s), pad_rows(v_s)
        kbar = _kbar_sample(cache_k, pt_flat, l, dec_batch, n_pages)
        sel = _sample_topk(q8, kbar)[:, :, :dec_seq, :TOP_K]
        attn_s = _sample_attn(q8, k8, v8, cache_k, cache_v, pt_flat, sel.reshape(-1), l, dec_seq, n_pages)
        attn_s = attn_s[:, :dec_seq].reshape(t_s, ATTN_WIDTH)
        attn = _moba_prompt(h, tail_base(attn_s.astype(BF16)), batch, seq)

        us_g = hs[:, u0:u0 + ssm_width].reshape(dec_batch, dec_seq, groups, CH)
        us_g = us_g.transpose(2, 0, 1, 3).reshape(groups, dec_batch, ws)
        ys_g, hn_g = _s5_sample(us_g, h0_s[l], tt_s[l], mt_s[l], nt_s[l], ar_s[l], ai_s[l], d_s[l])
        y_s = ys_g.reshape(groups, dec_batch, dec_seq, CH).transpose(1, 2, 0, 3).reshape(t_s, ssm_width)
        h3 = h.reshape(t_pad // L, L, h.shape[1])
        y3, hl4 = _s5_prompt(h3, tail_base(y_s).reshape(t_pad // L, L, ssm_width), perm, tt_b, mt_b, nt_b,
                             tabs['ar'], tabs['ai'], d_lane, l, batch, n_chunks, u0 // LANES)
        y_ssm = y3.reshape(t_pad, ssm_width)

        x = _mix_ln(attn, y_ssm, h, x, w_ao_b, w_gv_b, w_gg_b, w_out_b, gains[1], biases[1], l, alpha)
        x = _ffn_ln(x, f2g, f2u, f2d, gains[2], biases[2], l, alpha)

        kp_l.append(k3[:t_p].reshape(batch, seq, N_HEADS, HEAD_DIM))
        vp_l.append(v3[:t_p].reshape(batch, seq, N_HEADS, HEAD_DIM))
        hl = hl4.transpose(0, 2, 1, 3).reshape(batch, groups, 2 * SSM_STATE)
        hrp_l.append(hl[..., :SSM_STATE])
        hip_l.append(hl[..., SSM_STATE:])
        ks_l.append(k3[t_p:t_p + t_s].reshape(dec_batch, dec_seq, N_HEADS, HEAD_DIM))
        vs_l.append(v3[t_p:t_p + t_s].reshape(dec_batch, dec_seq, N_HEADS, HEAD_DIM))
        hn = hn_g.transpose(1, 0, 2)
        hrs_l.append(hn[..., :SSM_STATE])
        his_l.append(hn[..., SSM_STATE:])

    y_prompt = x[:t_p].reshape(batch, seq, d_model)
    y_sample = x[t_p:t_p + t_s].reshape(dec_batch, dec_seq, d_model)
    return (y_prompt, y_sample,
            jnp.stack(kp_l), jnp.stack(vp_l), jnp.stack(hrp_l), jnp.stack(hip_l),
            jnp.stack(ks_l), jnp.stack(vs_l), jnp.stack(hrs_l), jnp.stack(his_l))
```

```python
import functools

import jax
import jax.numpy as jnp
from jax import lax
from jax.experimental import pallas as pl
from jax.experimental.pallas import tpu as pltpu

N_HEADS = 8
HEAD_DIM = 128
ATTN_WIDTH = N_HEADS * HEAD_DIM
ROT_DIM = HEAD_DIM // 4
ROT_HALF = ROT_DIM // 2
ROPE_THETA = 500000.0
BLOCK = 256
TOP_K = 3
PAGE_SIZE = 128
PAGES_PER_BLOCK = BLOCK // PAGE_SIZE
SSM_GROUP = 16
SSM_STATE = 64
SSM_CHUNK = 16
LANES = 128
GROUPS_PER_LANE_BLOCK = LANES // SSM_GROUP
LN_EPS = 1e-5
NEG = -1e30

ROW_TILE = 528
VMEM_LIMIT = 56 * 1024 * 1024

F32 = jnp.float32
BF16 = jnp.bfloat16
HIGHEST = lax.Precision.HIGHEST


def _params(*sem):
    return pltpu.CompilerParams(dimension_semantics=sem, vmem_limit_bytes=VMEM_LIMIT)


def _layer_norm(r, g, b):
    mu = jnp.mean(r, axis=-1, keepdims=True)
    d = r - mu
    var = jnp.mean(d * d, axis=-1, keepdims=True)
    return d * lax.rsqrt(var + LN_EPS) * g + b


def _dot(a, b, precision=None):
    return jnp.dot(a, b, precision=precision, preferred_element_type=F32)


def _dot_nt(a, b, precision=None):
    return lax.dot_general(a, b, (((1,), (1,)), ((), ())), precision=precision,
                           preferred_element_type=F32)


KBAR_PAGES_PER_STEP = 8


def _ffn_ln_kernel(*refs, alpha, kbar_steps):
    n_pages = KBAR_PAGES_PER_STEP if kbar_steps else 0
    refs = refs[1:] if kbar_steps else refs
    x_ref, wg_ref, wu_ref, wd_ref, g_ref, b_ref = refs[:6]
    pages = refs[6:6 + n_pages]
    o_ref = refs[6 + n_pages]
    kbar_ref = refs[7 + n_pages] if kbar_steps else None
    xb_ref, acc_ref = refs[-2:]
    i = pl.program_id(0)
    j = pl.program_id(1)

    @pl.when(j == 0)
    def _():
        xb_ref[...] = x_ref[...].astype(BF16)
        acc_ref[...] = jnp.zeros_like(acc_ref)

    xb = xb_ref[...]
    gate = _dot(xb, wg_ref[...])
    up = _dot(xb, wu_ref[...])
    hid = (gate * jax.nn.sigmoid(gate) * up).astype(BF16)
    acc_ref[...] += _dot(hid, wd_ref[...])

    if kbar_steps:
        @pl.when(i * pl.num_programs(1) + j < kbar_steps)
        def _():
            for blk in range(n_pages // PAGES_PER_BLOCK):
                tot = None
                for p in range(PAGES_PER_BLOCK):
                    part = jnp.sum(pages[blk * PAGES_PER_BLOCK + p][...], axis=0)
                    tot = part if tot is None else tot + part
                kbar_ref[blk] = tot * (1.0 / BLOCK)

    @pl.when(j == pl.num_programs(1) - 1)
    def _():
        o_ref[...] = _layer_norm(alpha * x_ref[...] + 0.5 * acc_ref[...], g_ref[...], b_ref[...])


def _ffn_ln(x, wg, wu, wd, gain, bias, layer, alpha, tf=512, kbar_from=None):
    t, d = x.shape
    f = wg.shape[-1]
    tm = ROW_TILE
    grid = (t // tm, f // tf)
    in_specs = [
        pl.BlockSpec((tm, d), lambda i, j, *_: (i, 0)),
        pl.BlockSpec((None, d, tf), lambda i, j, *_: (layer, 0, j)),
        pl.BlockSpec((None, d, tf), lambda i, j, *_: (layer, 0, j)),
        pl.BlockSpec((None, tf, d), lambda i, j, *_: (layer, j, 0)),
        pl.BlockSpec((1, d), lambda i, j, *_: (0, 0)),
        pl.BlockSpec((1, d), lambda i, j, *_: (0, 0)),
    ]
    out_specs = pl.BlockSpec((tm, d), lambda i, j, *_: (i, 0))
    out_shape = jax.ShapeDtypeStruct((t, d), F32)
    scratch = [pltpu.VMEM((tm, d), BF16), pltpu.VMEM((tm, d), F32)]
    if kbar_from is None:
        return pl.pallas_call(
            functools.partial(_ffn_ln_kernel, alpha=alpha, kbar_steps=0),
            grid=grid, in_specs=in_specs, out_specs=out_specs, out_shape=out_shape, scratch_shapes=scratch,
            compiler_params=_params("parallel", "arbitrary"), name="ffn_ln",
        )(x, wg, wu, wd, gain, bias)

    cache_k, pt_flat, dec_batch, n_pages = kbar_from
    pps = KBAR_PAGES_PER_STEP
    kbar_steps = dec_batch * n_pages // pps
    steps_per_seq = n_pages // pps
    blocks_per_step = pps // PAGES_PER_BLOCK
    assert kbar_steps <= grid[0] * grid[1]

    def side_step(i, j):
        return jnp.minimum(i * grid[1] + j, kbar_steps - 1)

    def page_spec(p):
        return pl.BlockSpec((None, None, PAGE_SIZE, N_HEADS, HEAD_DIM),
                            lambda i, j, pt: (layer, pt[side_step(i, j) * pps + p], 0, 0, 0))

    kbar_spec = pl.BlockSpec(
        (None, blocks_per_step, N_HEADS, HEAD_DIM),
        lambda i, j, pt: (side_step(i, j) // steps_per_seq, side_step(i, j) % steps_per_seq, 0, 0))
    return pl.pallas_call(
        functools.partial(_ffn_ln_kernel, alpha=alpha, kbar_steps=kbar_steps),
        grid_spec=pltpu.PrefetchScalarGridSpec(
            num_scalar_prefetch=1, grid=grid,
            in_specs=in_specs + [page_spec(p) for p in range(pps)],
            out_specs=[out_specs, kbar_spec], scratch_shapes=scratch),
        out_shape=[out_shape, jax.ShapeDtypeStruct((dec_batch, n_pages // PAGES_PER_BLOCK, N_HEADS, HEAD_DIM), F32)],
        compiler_params=_params("arbitrary", "arbitrary"), name="ffn_ln_kbar",
    )(pt_flat, x, wg, wu, wd, gain, bias, *([cache_k] * pps))


def _inproj_kernel(x_ref, w_ref, c_ref, s1_ref, s2_ref, o_ref, k3_ref, v3_ref, xb_ref):
    j = pl.program_id(1)

    @pl.when(j == 0)
    def _():
        xb_ref[...] = x_ref[...].astype(BF16)

    def heads():
        half = N_HEADS // 2
        for part in range(2):
            y = _dot(xb_ref[...], w_ref[:, part * half * HEAD_DIM:(part + 1) * half * HEAD_DIM])
            for hl in range(half):
                yield part * half + hl, y[:, hl * HEAD_DIM:(hl + 1) * HEAD_DIM]

    def rope(yh):
        return (yh * c_ref[...]
                + pltpu.roll(yh, ROT_HALF, 1) * s1_ref[...]
                + pltpu.roll(yh, HEAD_DIM - ROT_HALF, 1) * s2_ref[...])

    @pl.when(j == 0)
    def _():
        for hh, yh in heads():
            o_ref[:, hh * HEAD_DIM:(hh + 1) * HEAD_DIM] = rope(yh)

    @pl.when(j == 1)
    def _():
        for hh, yh in heads():
            kh = rope(yh)
            o_ref[:, hh * HEAD_DIM:(hh + 1) * HEAD_DIM] = kh
            k3_ref[:, hh, :] = kh

    @pl.when(j == 2)
    def _():
        for hh, yh in heads():
            o_ref[:, hh * HEAD_DIM:(hh + 1) * HEAD_DIM] = yh
            v3_ref[:, hh, :] = yh

    @pl.when(j > 2)
    def _():
        o_ref[...] = _dot(xb_ref[...], w_ref[...])


def _in_proj(x, w_in, rope_c, rope_s1, rope_s2, layer):
    t, d = x.shape
    n = w_in.shape[-1]
    tm, tn = ROW_TILE, ATTN_WIDTH
    kv_spec = pl.BlockSpec((tm, N_HEADS, HEAD_DIM), lambda i, j: (i, 0, 0))
    kv_shape = jax.ShapeDtypeStruct((t, N_HEADS, HEAD_DIM), F32)
    return pl.pallas_call(
        _inproj_kernel,
        grid=(t // tm, n // tn),
        in_specs=[
            pl.BlockSpec((tm, d), lambda i, j: (i, 0)),
            pl.BlockSpec((None, d, tn), lambda i, j: (layer, 0, j)),
            pl.BlockSpec((tm, HEAD_DIM), lambda i, j: (i, 0)),
            pl.BlockSpec((tm, HEAD_DIM), lambda i, j: (i, 0)),
            pl.BlockSpec((tm, HEAD_DIM), lambda i, j: (i, 0)),
        ],
        out_specs=[pl.BlockSpec((tm, tn), lambda i, j: (i, j)), kv_spec, kv_spec],
        out_shape=[jax.ShapeDtypeStruct((t, n), F32), kv_shape, kv_shape],
        scratch_shapes=[pltpu.VMEM((tm, d), BF16)],
        compiler_params=_params("parallel", "arbitrary"),
        name="in_proj_rope",
    )(x, w_in, rope_c, rope_s1, rope_s2)


def _rope_tables(pos):
    inv = jnp.power(ROPE_THETA, -jnp.arange(ROT_HALF, dtype=F32) * 2.0 / ROT_DIM)
    ang = pos.astype(F32)[:, None] * inv[None, :]
    cos, sin = jnp.cos(ang), jnp.sin(ang)
    t = pos.shape[0]
    ones = jnp.ones((t, HEAD_DIM - ROT_DIM), F32)
    zeros_h = jnp.zeros((t, ROT_HALF), F32)
    zeros_r = jnp.zeros((t, HEAD_DIM - ROT_DIM), F32)
    c = jnp.concatenate([cos, cos, ones], axis=1)
    s1 = jnp.concatenate([zeros_h, sin, zeros_r], axis=1)
    s2 = jnp.concatenate([-sin, zeros_h, zeros_r], axis=1)
    return c, s1, s2


MOBA_HEADS_PER_STEP = 2


def _moba_prompt_kernel(q_ref, k_ref, v_ref, base_ref, o_ref, kb_ref, vt_ref, kbar_ref, *, n_blocks, scale):
    del base_ref
    c = pl.program_id(2)
    lanes_of = lambda hd: slice(hd * HEAD_DIM, (hd + 1) * HEAD_DIM)

    @pl.when(c == 0)
    def _():
        for n in range(n_blocks):
            kbar_ref[n:n + 1, :] = jnp.mean(k_ref[n * BLOCK:(n + 1) * BLOCK, :], axis=0, keepdims=True)
        kb_ref[...] = k_ref[...].astype(BF16)
        vt_ref[...] = v_ref[...].T.astype(BF16)

    blk_id = lax.broadcasted_iota(jnp.int32, (n_blocks, BLOCK), 0)
    key_row = lax.broadcasted_iota(jnp.int32, (BLOCK, BLOCK), 0)
    qry_col = lax.broadcasted_iota(jnp.int32, (BLOCK, BLOCK), 1)
    causal = key_row <= qry_col

    def pick_blocks(q, kbar):
        gates = _dot_nt(kbar, q, precision=HIGHEST)
        rank = jnp.zeros((n_blocks, BLOCK), F32)
        for m in range(n_blocks):
            gm = gates[m:m + 1, :]
            past = jnp.where(m < c, 1.0, 0.0)
            ge = jnp.where(gm >= gates, past, 0.0)
            gt = jnp.where(gm > gates, past, 0.0)
            rank = rank + jnp.where(blk_id > m, ge, gt)
        return jnp.where(rank < TOP_K, jnp.where(blk_id < c, 1.0, 0.0), 0.0)

    def attend_head(hd, sel, qb, nb, first_own):
        s_t = _dot_nt(kb_ref[0:nb * BLOCK, lanes_of(hd)], qb)
        tiles, keeps = [], []
        for n in range(nb):
            keep = sel[n:n + 1, :]
            if n >= first_own:
                keep = keep + jnp.where(causal, jnp.where(c == n, 1.0, 0.0), 0.0)
            keeps.append(keep > 0.5)
            tiles.append(s_t[n * BLOCK:(n + 1) * BLOCK, :])
        m = jnp.where(keeps[0], tiles[0], NEG).max(axis=0, keepdims=True)
        for keep, tile in zip(keeps[1:], tiles[1:]):
            m = jnp.maximum(m, jnp.where(keep, tile, NEG).max(axis=0, keepdims=True))
        probs = [jnp.exp(jnp.where(keep, tile - m, NEG)) for keep, tile in zip(keeps, tiles)]
        denom = probs[0].sum(axis=0, keepdims=True)
        for p in probs[1:]:
            denom = denom + p.sum(axis=0, keepdims=True)
        p_t = jnp.concatenate([p.astype(BF16) for p in probs], axis=0)
        o_t = _dot(vt_ref[lanes_of(hd), 0:nb * BLOCK], p_t) * (1.0 / denom)
        return o_t.T.astype(o_ref.dtype)

    heads = range(MOBA_HEADS_PER_STEP)
    qs = [q_ref[:, lanes_of(hd)] for hd in heads]
    sels = [pick_blocks(qs[hd], kbar_ref[:, lanes_of(hd)]) for hd in heads]
    qbs = [(q * scale).astype(BF16) for q in qs]

    def attend(nb, first_own):
        o_ref[...] = jnp.concatenate([attend_head(hd, sels[hd], qbs[hd], nb, first_own) for hd in heads], axis=1)

    step = 2
    for nb in range(step, n_blocks + 1, step):
        pl.when(jnp.logical_and(c < nb, c >= nb - step))(functools.partial(attend, nb, nb - step))


def _moba_prompt(h, base, batch, seq):
    n_blocks = seq // BLOCK
    width = MOBA_HEADS_PER_STEP * HEAD_DIM
    groups = N_HEADS // MOBA_HEADS_PER_STEP
    return pl.pallas_call(
        functools.partial(_moba_prompt_kernel, n_blocks=n_blocks, scale=HEAD_DIM ** -0.5),
        grid=(batch, groups, n_blocks),
        in_specs=[
            pl.BlockSpec((BLOCK, width), lambda b, hg, c: (b * n_blocks + c, hg)),
            pl.BlockSpec((seq, width), lambda b, hg, c: (b, groups + hg)),
            pl.BlockSpec((seq, width), lambda b, hg, c: (b, 2 * groups + hg)),
            pl.BlockSpec(memory_space=pl.ANY),
        ],
        out_specs=pl.BlockSpec((BLOCK, width), lambda b, hg, c: (b * n_blocks + c, hg)),
        out_shape=jax.ShapeDtypeStruct(base.shape, base.dtype),
        input_output_aliases={3: 0},
        scratch_shapes=[pltpu.VMEM((seq, width), BF16), pltpu.VMEM((width, seq), BF16),
                        pltpu.VMEM((n_blocks, width), F32)],
        compiler_params=_params("parallel", "parallel", "arbitrary"),
        name="moba_prompt",
    )(h, h, h, base)


def _sample_topk_kernel(q_ref, kbar_ref, sel_ref):
    n_past = kbar_ref.shape[0]
    rows = q_ref.shape[0]
    lane_g = lax.broadcasted_iota(jnp.int32, (rows, n_past), 1).astype(F32)
    lane_o = lax.broadcasted_iota(jnp.int32, (rows, HEAD_DIM), 1)
    for hh in range(N_HEADS):
        qh = q_ref[:, hh * HEAD_DIM:(hh + 1) * HEAD_DIM]
        g = _dot_nt(qh, kbar_ref[:, hh, :], precision=HIGHEST)
        out = jnp.zeros((rows, HEAD_DIM), jnp.int32)
        for kk in range(TOP_K):
            best = jnp.max(g, axis=-1, keepdims=True)
            idx = jnp.min(jnp.where(g == best, lane_g, float(n_past)), axis=-1, keepdims=True)
            out = jnp.where(lane_o == kk, idx.astype(jnp.int32), out)
            g = jnp.where(lane_g == idx, -jnp.inf, g)
        sel_ref[hh] = out


def _sample_topk(q8, kbar):
    dec_batch, rows, width = q8.shape
    n_past = kbar.shape[1]
    return pl.pallas_call(
        _sample_topk_kernel,
        grid=(dec_batch,),
        in_specs=[
            pl.BlockSpec((None, rows, width), lambda b: (b, 0, 0)),
            pl.BlockSpec((None, n_past, N_HEADS, HEAD_DIM), lambda b: (b, 0, 0, 0)),
        ],
        out_specs=pl.BlockSpec((None, N_HEADS, rows, HEAD_DIM), lambda b: (b, 0, 0, 0)),
        out_shape=jax.ShapeDtypeStruct((dec_batch, N_HEADS, rows, HEAD_DIM), jnp.int32),
        compiler_params=_params("parallel"),
        name="sample_topk",
    )(q8, kbar)


def _sample_attn_kernel(pt_ref, sel_ref, q_ref, kn_ref, vn_ref, ck_hbm, cv_hbm, o_ref, kbuf, vbuf, sem,
                        *, layer, dec_seq, n_pages, scale):
    b = pl.program_id(0)
    hd = pl.program_id(1)
    step = b * N_HEADS + hd
    n_steps = pl.num_programs(0) * N_HEADS
    slot = step % 2
    sel_per_head = dec_seq * TOP_K
    n_blk = dec_seq * TOP_K

    def page_copies(bb, hh, sl):
        copies = []
        for i in range(n_blk):
            blk = sel_ref[(bb * N_HEADS + hh) * sel_per_head + i]
            for p in range(PAGES_PER_BLOCK):
                page = pt_ref[bb * n_pages + blk * PAGES_PER_BLOCK + p]
                dst = i * PAGES_PER_BLOCK + p
                copies.append(pltpu.make_async_copy(ck_hbm.at[layer, page, :, hh, :], kbuf.at[sl, dst], sem.at[0, sl]))
                copies.append(pltpu.make_async_copy(cv_hbm.at[layer, page, :, hh, :], vbuf.at[sl, dst], sem.at[1, sl]))
        return copies

    @pl.when(step == 0)
    def _():
        for cp in page_copies(b, hd, slot):
            cp.start()

    @pl.when(step + 1 < n_steps)
    def _():
        nxt = step + 1
        for cp in page_copies(nxt // N_HEADS, nxt % N_HEADS, 1 - slot):
            cp.start()

    for cp in page_copies(b, hd, slot):
        cp.wait()

    rows = q_ref.shape[0]
    qb = q_ref[...].astype(BF16)
    s_own_all = _dot_nt(qb, kn_ref[...].astype(BF16)) * scale
    vn = vn_ref[...].astype(BF16)
    col = lax.broadcasted_iota(jnp.int32, (1, rows), 1)
    o_ref[...] = jnp.zeros_like(o_ref)
    for qi in range(dec_seq):
        s_sel, v_sel = [], []
        for kk in range(TOP_K):
            base = (qi * TOP_K + kk) * PAGES_PER_BLOCK
            kb = jnp.concatenate([kbuf[slot, base + p] for p in range(PAGES_PER_BLOCK)], axis=0)
            vb = jnp.concatenate([vbuf[slot, base + p] for p in range(PAGES_PER_BLOCK)], axis=0)
            s_sel.append(_dot_nt(qb, kb.astype(BF16))[qi:qi + 1, :] * scale)
            v_sel.append(vb.astype(BF16))
        s_own = jnp.where(col <= qi, s_own_all[qi:qi + 1, :], NEG)
        m = jnp.max(s_own, axis=-1, keepdims=True)
        for s in s_sel:
            m = jnp.maximum(m, jnp.max(s, axis=-1, keepdims=True))
        p_own = jnp.exp(s_own - m)
        denom = jnp.sum(p_own, axis=-1, keepdims=True)
        out = _dot(jnp.broadcast_to(p_own, (rows, rows)).astype(BF16), vn)[0:1, :]
        for s, vb in zip(s_sel, v_sel):
            p = jnp.exp(s - m)
            denom = denom + jnp.sum(p, axis=-1, keepdims=True)
            out = out + _dot(jnp.broadcast_to(p, (rows, BLOCK)).astype(BF16), vb)[0:1, :]
        o_ref[qi:qi + 1, :] = out / denom


def _sample_attn(q8, k8, v8, cache_k, cache_v, pt_flat, sel_flat, layer, dec_seq, n_pages):
    dec_batch, rows, _ = q8.shape
    n_sel_pages = dec_seq * TOP_K * PAGES_PER_BLOCK
    head_spec = pl.BlockSpec((None, rows, HEAD_DIM), lambda b, hh, pt, sel: (b, 0, hh))
    hbm = pl.BlockSpec(memory_space=pl.ANY)
    return pl.pallas_call(
        functools.partial(_sample_attn_kernel, layer=layer, dec_seq=dec_seq, n_pages=n_pages,
                          scale=HEAD_DIM ** -0.5),
        grid_spec=pltpu.PrefetchScalarGridSpec(
            num_scalar_prefetch=2,
            grid=(dec_batch, N_HEADS),
            in_specs=[head_spec, head_spec, head_spec, hbm, hbm],
            out_specs=head_spec,
            scratch_shapes=[pltpu.VMEM((2, n_sel_pages, PAGE_SIZE, HEAD_DIM), F32),
                            pltpu.VMEM((2, n_sel_pages, PAGE_SIZE, HEAD_DIM), F32),
                            pltpu.SemaphoreType.DMA((2, 2))],
        ),
        out_shape=jax.ShapeDtypeStruct(q8.shape, F32),
        compiler_params=_params("arbitrary", "arbitrary"),
        name="sample_attn",
    )(pt_flat, sel_flat, q8, k8, v8, cache_k, cache_v)


def _s5_tables(a_re, a_im, log_dt, b_re, b_im, c_re, c_im):
    L = SSM_CHUNK
    lam = lax.complex(a_re, a_im)
    dt = jnp.exp(log_dt)[..., None]
    lam_bar = jnp.exp(lam * dt)
    b_bar = ((lam_bar - 1.0) / lam)[..., None] * lax.complex(b_re, b_im)
    c_mat = lax.complex(c_re, c_im)
    pw = [jnp.ones_like(lam_bar)]
    for _ in range(L):
        pw.append(pw[-1] * lam_bar)
    pows = jnp.stack(pw, axis=2)
    z = (c_mat.transpose(0, 1, 3, 2)[:, :, :, None, :]
         * pows.transpose(0, 1, 3, 2)[:, :, :, :, None])
    dd, gg = z.shape[0], z.shape[1]
    z = z.reshape(dd, gg, SSM_STATE, (L + 1) * SSM_GROUP)
    zt = jnp.concatenate([jnp.real(z), -jnp.imag(z)], axis=2)
    bst = jnp.concatenate([jnp.real(b_bar), jnp.imag(b_bar)], axis=2)
    r0 = jnp.einsum('dgpi,dgpn->dgin', bst, zt[..., :L * SSM_GROUP], precision=HIGHEST)
    klag = r0.reshape(dd, gg, SSM_GROUP, L, SSM_GROUP).transpose(0, 1, 3, 2, 4)
    lag = jnp.arange(L)[None, :] - jnp.arange(L)[:, None]
    tt = jnp.where((lag >= 0)[None, None, :, :, None, None], klag[:, :, jnp.clip(lag, 0, L - 1)], 0.0)
    tt = tt.transpose(0, 1, 2, 4, 3, 5).reshape(dd, gg, L * SSM_GROUP, L * SSM_GROUP)
    nt = zt[..., SSM_GROUP:]
    mcx = pows[:, :, L - 1::-1][:, :, :, :, None] * b_bar[:, :, None]
    mcx = mcx.transpose(0, 1, 2, 4, 3).reshape(dd, gg, L * SSM_GROUP, SSM_STATE)
    mt = jnp.concatenate([jnp.real(mcx), jnp.imag(mcx)], axis=-1)

    def lanes(zc):
        return (jnp.concatenate([jnp.real(zc), jnp.real(zc)], axis=-1),
                jnp.concatenate([-jnp.imag(zc), jnp.imag(zc)], axis=-1))

    steps = []
    cur = pows[:, :, L]
    for _ in range(8):
        steps.append(cur)
        cur = cur * cur
    ar, ai = lanes(jnp.stack(steps, axis=2))
    return dict(tt=tt, mt=mt, nt=nt, ar=ar, ai=ai, pows=pows)


def _chunk_permutation():
    n = SSM_CHUNK * LANES
    src = jnp.arange(n)
    t, g, ch = src // LANES, (src % LANES) // SSM_GROUP, src % SSM_GROUP
    dst = g * (SSM_CHUNK * SSM_GROUP) + t * SSM_GROUP + ch
    return (dst[:, None] == jnp.arange(n)[None, :]).astype(BF16)


def _s5_prompt_kernel(x_ref, perm_ref, tt_ref, mt_ref, nt_ref, ar_ref, ai_ref, d_ref, base_ref, y_ref, hl_ref,
                      *, n_seq, n_chunks):
    del base_ref
    L = SSM_CHUNK
    gw = L * SSM_GROUP
    rows = n_seq * n_chunks
    u_tok = jnp.concatenate([x_ref[:, t, :].astype(BF16) for t in range(L)], axis=1)
    u_grp = _dot(u_tok, perm_ref[...]).astype(BF16)
    kpos = lax.broadcasted_iota(jnp.int32, (rows, 2 * SSM_STATE), 0) % n_chunks
    y_parts = []
    for g in range(GROUPS_PER_LANE_BLOCK):
        ub = u_grp[:, g * gw:(g + 1) * gw]
        w = _dot(ub, mt_ref[g])
        stride, i = 1, 0
        while stride < n_chunks:
            sh = jnp.where(kpos >= stride, pltpu.roll(w, stride, 0), 0.0)
            w = w + sh * ar_ref[g, i:i + 1, :] + pltpu.roll(sh, SSM_STATE, 1) * ai_ref[g, i:i + 1, :]
            stride, i = stride * 2, i + 1
        h_in = jnp.where(kpos >= 1, pltpu.roll(w, 1, 0), 0.0)
        y_parts.append((_dot(ub, tt_ref[g]) + _dot(h_in.astype(BF16), nt_ref[g])).astype(BF16))
        for n in range(n_seq):
            hl_ref[g, n:n + 1, :] = w[(n + 1) * n_chunks - 1:(n + 1) * n_chunks, :]
    y_tok = _dot_nt(jnp.concatenate(y_parts, axis=1), perm_ref[...])
    for t in range(L):
        y_ref[:, t, :] = y_tok[:, t * LANES:(t + 1) * LANES] + d_ref[...] * x_ref[:, t, :]


def _s5_prompt(h3, base3, perm, tt, mt, nt, ar, ai, d_lane, layer, n_seq_total, n_chunks, u_lane_block0,
               seq_per_step=2):
    groups = tt.shape[1]
    lane_blocks = groups // GROUPS_PER_LANE_BLOCK
    rows = seq_per_step * n_chunks
    p2 = 2 * SSM_STATE
    gw = SSM_CHUNK * SSM_GROUP
    gpb = GROUPS_PER_LANE_BLOCK
    n_perm = SSM_CHUNK * LANES
    return pl.pallas_call(
        functools.partial(_s5_prompt_kernel, n_seq=seq_per_step, n_chunks=n_chunks),
        grid=(n_seq_total // seq_per_step, lane_blocks),
        in_specs=[
            pl.BlockSpec((rows, SSM_CHUNK, LANES), lambda s, g: (s, 0, u_lane_block0 + g)),
            pl.BlockSpec((n_perm, n_perm), lambda s, g: (0, 0)),
            pl.BlockSpec((None, gpb, gw, gw), lambda s, g: (layer, g, 0, 0)),
            pl.BlockSpec((None, gpb, gw, p2), lambda s, g: (layer, g, 0, 0)),
            pl.BlockSpec((None, gpb, p2, gw), lambda s, g: (layer, g, 0, 0)),
            pl.BlockSpec((None, gpb, 8, p2), lambda s, g: (layer, g, 0, 0)),
            pl.BlockSpec((None, gpb, 8, p2), lambda s, g: (layer, g, 0, 0)),
            pl.BlockSpec((None, None, 1, LANES), lambda s, g: (layer, g, 0, 0)),
            pl.BlockSpec(memory_space=pl.ANY),
        ],
        out_specs=[
            pl.BlockSpec((rows, SSM_CHUNK, LANES), lambda s, g: (s, 0, g)),
            pl.BlockSpec((None, gpb, seq_per_step, p2), lambda s, g: (s, g, 0, 0)),
        ],
        out_shape=[
            jax.ShapeDtypeStruct(base3.shape, base3.dtype),
            jax.ShapeDtypeStruct((n_seq_total // seq_per_step, groups, seq_per_step, p2), F32),
        ],
        input_output_aliases={8: 0},
        compiler_params=_params("parallel", "arbitrary"),
        name="s5_prompt",
    )(h3, perm, tt, mt, nt, ar, ai, d_lane, base3)


def _s5_sample_kernel(u_ref, h0_ref, tt_ref, mt_ref, nt_ref, ar_ref, ai_ref, d_ref, y_ref, hn_ref):
    groups = u_ref.shape[0]

    def body(g, carry):
        u = u_ref[g]
        h0 = h0_ref[g]
        y_ref[g] = _dot(u, tt_ref[g], HIGHEST) + _dot(h0, nt_ref[g], HIGHEST) + d_ref[g] * u
        hn_ref[g] = (h0 * ar_ref[g] + pltpu.roll(h0, SSM_STATE, 1) * ai_ref[g]
                     + _dot(u, mt_ref[g], HIGHEST))
        return carry

    lax.fori_loop(0, groups, body, 0)


def _s5_sample(u_g, h0_g, tt_s, mt_s, nt_s, ar_s, ai_s, d_s):
    args = (u_g, h0_g, tt_s, mt_s, nt_s, ar_s, ai_s, d_s)
    full = lambda a: pl.BlockSpec(a.shape, lambda i, nd=a.ndim: (0,) * nd)
    return pl.pallas_call(
        _s5_sample_kernel,
        grid=(1,),
        in_specs=[full(a) for a in args],
        out_specs=[full(u_g), full(h0_g)],
        out_shape=[jax.ShapeDtypeStruct(u_g.shape, F32), jax.ShapeDtypeStruct(h0_g.shape, F32)],
        compiler_params=_params("arbitrary"),
        name="s5_sample",
    )(*args)


def _mix_ln_kernel(a_ref, ys_ref, ga_ref, gb_ref, x_ref, wa_ref, wv_ref, wg_ref, wo_ref, g_ref, b_ref,
                   o_ref, zb_ref, acc_ref, *, alpha):
    j = pl.program_id(1)

    @pl.when(j == 0)
    def _():
        zb_ref[...] = jax.nn.gelu(ys_ref[...]).astype(BF16)
        acc_ref[...] = jnp.zeros_like(acc_ref)

    zb = zb_ref[...]
    y_a = _dot(a_ref[...], wa_ref[...])
    y_b = _dot(zb, wv_ref[...]) * jax.nn.sigmoid(_dot(zb, wg_ref[...]))
    mixed = jax.nn.sigmoid(ga_ref[...]) * y_a + jax.nn.sigmoid(gb_ref[...]) * y_b
    acc_ref[...] += _dot(mixed.astype(BF16), wo_ref[...])

    @pl.when(j == pl.num_programs(1) - 1)
    def _():
        o_ref[...] = _layer_norm(alpha * x_ref[...] + acc_ref[...], g_ref[...], b_ref[...])


def _mix_ln(attn, y_ssm, h, x, wa, wv, wg, wo, gain, bias, layer, alpha, tn=512):
    t, d = x.shape
    aw = attn.shape[1]
    sw = y_ssm.shape[1]
    tm = ROW_TILE
    ga_blk = (3 * ATTN_WIDTH + sw) // tn
    gb_blk = ga_blk + d // tn
    return pl.pallas_call(
        functools.partial(_mix_ln_kernel, alpha=alpha),
        grid=(t // tm, d // tn),
        in_specs=[
            pl.BlockSpec((tm, aw), lambda i, j: (i, 0)),
            pl.BlockSpec((tm, sw), lambda i, j: (i, 0)),
            pl.BlockSpec((tm, tn), lambda i, j: (i, ga_blk + j)),
            pl.BlockSpec((tm, tn), lambda i, j: (i, gb_blk + j)),
            pl.BlockSpec((tm, d), lambda i, j: (i, 0)),
            pl.BlockSpec((None, aw, tn), lambda i, j: (layer, 0, j)),
            pl.BlockSpec((None, sw, tn), lambda i, j: (layer, 0, j)),
            pl.BlockSpec((None, sw, tn), lambda i, j: (layer, 0, j)),
            pl.BlockSpec((None, tn, d), lambda i, j: (layer, j, 0)),
            pl.BlockSpec((1, d), lambda i, j: (0, 0)),
            pl.BlockSpec((1, d), lambda i, j: (0, 0)),
        ],
        out_specs=pl.BlockSpec((tm, d), lambda i, j: (i, 0)),
        out_shape=jax.ShapeDtypeStruct((t, d), F32),
        scratch_shapes=[pltpu.VMEM((tm, sw), BF16), pltpu.VMEM((tm, d), F32)],
        compiler_params=_params("parallel", "arbitrary"),
        name="mix_ln",
    )(attn, y_ssm, h, h, x, wa, wv, wg, wo, gain, bias)


def kernel(x_prompt, x_sample, cache_k, cache_v, state_ssm_re, state_ssm_im, page_table, w_in, w_attn_out, w_glu_v, w_glu_g, w_out, ssm_a_re, ssm_a_im, ssm_log_dt, ssm_b_re, ssm_b_im, ssm_c_re, ssm_c_im, ssm_d, ffn1_gate, ffn1_up, ffn1_down, ffn2_gate, ffn2_up, ffn2_down, ln_gain, ln_bias):
    batch, seq, d_model = x_prompt.shape
    dec_batch, dec_seq, _ = x_sample.shape
    depth = w_in.shape[0]
    n_pages = page_table.shape[1]
    past_len = n_pages * PAGE_SIZE
    ssm_width = ssm_d.shape[1]
    groups = ssm_width // SSM_GROUP
    L, CH = SSM_CHUNK, SSM_GROUP
    n_chunks = seq // L
    t_p = batch * seq
    t_s = dec_batch * dec_seq
    t_pad = -(-(t_p + t_s) // ROW_TILE) * ROW_TILE
    t_tail = t_pad - t_p
    assert past_len % BLOCK == 0 and past_len // BLOCK >= TOP_K
    assert seq % (2 * BLOCK) == 0 and t_pad % L == 0 and batch % 2 == 0
    assert dec_seq <= 8 and dec_seq <= L and n_pages % KBAR_PAGES_PER_STEP == 0
    assert (3 * ATTN_WIDTH) % LANES == 0 and groups % GROUPS_PER_LANE_BLOCK == 0
    alpha = (2.0 * depth) ** 0.25

    x = jnp.concatenate([x_prompt.reshape(t_p, d_model), x_sample.reshape(t_s, d_model),
                         jnp.zeros((t_tail - t_s, d_model), F32)], axis=0)

    pos = jnp.concatenate([jnp.tile(jnp.arange(seq, dtype=jnp.int32), batch),
                           jnp.tile(past_len + jnp.arange(dec_seq, dtype=jnp.int32), dec_batch),
                           jnp.zeros((t_tail - t_s,), jnp.int32)])
    rope_c, rope_s1, rope_s2 = _rope_tables(pos)

    w_in_b, w_ao_b, w_gv_b, w_gg_b, w_out_b = (w.astype(BF16) for w in (w_in, w_attn_out, w_glu_v, w_glu_g, w_out))
    f1g, f1u, f1d, f2g, f2u, f2d = (w.astype(BF16) for w in
                                    (ffn1_gate, ffn1_up, ffn1_down, ffn2_gate, ffn2_up, ffn2_down))

    tabs = _s5_tables(ssm_a_re, ssm_a_im, ssm_log_dt, ssm_b_re, ssm_b_im, ssm_c_re, ssm_c_im)
    tt_b, mt_b, nt_b = tabs['tt'].astype(BF16), tabs['mt'].astype(BF16), tabs['nt'].astype(BF16)
    perm = _chunk_permutation()
    d_lane = ssm_d.reshape(depth, ssm_width // LANES, 1, LANES)
    d_g = ssm_d.reshape(depth, groups, 1, CH)
    ws = dec_seq * CH
    tt_s = tabs['tt'][:, :, :ws, :ws]
    nt_s = tabs['nt'][:, :, :, :ws]
    mt_s = tabs['mt'][:, :, (L - dec_seq) * CH:, :]
    lam_ds = tabs['pows'][:, :, dec_seq]
    ar_s = jnp.concatenate([jnp.real(lam_ds), jnp.real(lam_ds)], axis=-1)[:, :, None, :]
    ai_s = jnp.concatenate([-jnp.imag(lam_ds), jnp.imag(lam_ds)], axis=-1)[:, :, None, :]
    d_s = jnp.tile(d_g, (1, 1, 1, dec_seq))
    h0_s = jnp.concatenate([state_ssm_re, state_ssm_im], axis=-1).transpose(0, 2, 1, 3)

    pt_flat = page_table.reshape(-1)
    row_pad = 8 - dec_seq

    def pad_rows(a):
        return jnp.pad(a, ((0, 0), (0, row_pad), (0, 0)))

    def tail_base(tail_rows):
        width, dtype = tail_rows.shape[1], tail_rows.dtype
        return jnp.concatenate([jnp.zeros((t_p, width), dtype), tail_rows,
                                jnp.zeros((t_tail - t_s, width), dtype)], axis=0)

    u0 = 3 * ATTN_WIDTH
    kp_l, vp_l, hrp_l, hip_l, ks_l, vs_l, hrs_l, his_l = ([] for _ in range(8))
    for l in range(depth):
        gains = [ln_gain[l, i][None, :] for i in range(3)]
        biases = [ln_bias[l, i][None, :] for i in range(3)]
        x, kbar = _ffn_ln(x, f1g, f1u, f1d, gains[0], biases[0], l, alpha,
                          kbar_from=(cache_k, pt_flat, dec_batch, n_pages))
        h, k3, v3 = _in_proj(x, w_in_b, rope_c, rope_s1, rope_s2, l)

        hs = h[t_p:t_p + t_s]
        q_s = hs[:, :ATTN_WIDTH].reshape(dec_batch, dec_seq, ATTN_WIDTH)
        k_s = hs[:, ATTN_WIDTH:2 * ATTN_WIDTH].reshape(dec_batch, dec_seq, ATTN_WIDTH)
        v_s = hs[:, 2 * ATTN_WIDTH:3 * ATTN_WIDTH].reshape(dec_batch, dec_seq, ATTN_WIDTH)
        q8, k8, v8 = pad_rows(q_s), pad_rows(k_s), pad_rows(v_s)
        sel = _sample_topk(q8, kbar)[:, :, :dec_seq, :TOP_K]
        attn_s = _sample_attn(q8, k8, v8, cache_k, cache_v, pt_flat, sel.reshape(-1), l, dec_seq, n_pages)
        attn_s = attn_s[:, :dec_seq].reshape(t_s, ATTN_WIDTH)
        attn = _moba_prompt(h, tail_base(attn_s.astype(BF16)), batch, seq)

        us_g = hs[:, u0:u0 + ssm_width].reshape(dec_batch, dec_seq, groups, CH)
        us_g = us_g.transpose(2, 0, 1, 3).reshape(groups, dec_batch, ws)
        ys_g, hn_g = _s5_sample(us_g, h0_s[l], tt_s[l], mt_s[l], nt_s[l], ar_s[l], ai_s[l], d_s[l])
        y_s = ys_g.reshape(groups, dec_batch, dec_seq, CH).transpose(1, 2, 0, 3).reshape(t_s, ssm_width)
        h3 = h.reshape(t_pad // L, L, h.shape[1])
        y3, hl4 = _s5_prompt(h3, tail_base(y_s).reshape(t_pad // L, L, ssm_width), perm, tt_b, mt_b, nt_b,
                             tabs['ar'], tabs['ai'], d_lane, l, batch, n_chunks, u0 // LANES)
        y_ssm = y3.reshape(t_pad, ssm_width)

        x = _mix_ln(attn, y_ssm, h, x, w_ao_b, w_gv_b, w_gg_b, w_out_b, gains[1], biases[1], l, alpha)
        x = _ffn_ln(x, f2g, f2u, f2d, gains[2], biases[2], l, alpha)

        kp_l.append(k3[:t_p].reshape(batch, seq, N_HEADS, HEAD_DIM))
        vp_l.append(v3[:t_p].reshape(batch, seq, N_HEADS, HEAD_DIM))
        hl = hl4.transpose(0, 2, 1, 3).reshape(batch, groups, 2 * SSM_STATE)
        hrp_l.append(hl[..., :SSM_STATE])
        hip_l.append(hl[..., SSM_STATE:])
        ks_l.append(k3[t_p:t_p + t_s].reshape(dec_batch, dec_seq, N_HEADS, HEAD_DIM))
        vs_l.append(v3[t_p:t_p + t_s].reshape(dec_batch, dec_seq, N_HEADS, HEAD_DIM))
        hn = hn_g.transpose(1, 0, 2)
        hrs_l.append(hn[..., :SSM_STATE])
        his_l.append(hn[..., SSM_STATE:])

    y_prompt = x[:t_p].reshape(batch, seq, d_model)
    y_sample = x[t_p:t_p + t_s].reshape(dec_batch, dec_seq, d_model)
    return (y_prompt, y_sample,
            jnp.stack(kp_l), jnp.stack(vp_l), jnp.stack(hrp_l), jnp.stack(hip_l),
            jnp.stack(ks_l), jnp.stack(vs_l), jnp.stack(hrs_l), jnp.stack(his_l))
```

```python
import functools

import jax
import jax.numpy as jnp
from jax import lax
from jax.experimental import pallas as pl
from jax.experimental.pallas import tpu as pltpu

N_HEADS = 8
HEAD_DIM = 128
ATTN_WIDTH = N_HEADS * HEAD_DIM
ROT_DIM = HEAD_DIM // 4
ROT_HALF = ROT_DIM // 2
ROPE_THETA = 500000.0
BLOCK = 256
TOP_K = 3
PAGE_SIZE = 128
PAGES_PER_BLOCK = BLOCK // PAGE_SIZE
SSM_GROUP = 16
SSM_STATE = 64
SSM_CHUNK = 16
LANES = 128
GROUPS_PER_LANE_BLOCK = LANES // SSM_GROUP
LN_EPS = 1e-5
NEG = -1e30

ROW_TILE = 528
DENSE_ROW_TILE = 2 * ROW_TILE
VMEM_LIMIT = 56 * 1024 * 1024

F32 = jnp.float32
BF16 = jnp.bfloat16
HIGHEST = lax.Precision.HIGHEST


def _params(*sem):
    return pltpu.CompilerParams(dimension_semantics=sem, vmem_limit_bytes=VMEM_LIMIT)


def _layer_norm(r, g, b):
    mu = jnp.mean(r, axis=-1, keepdims=True)
    d = r - mu
    var = jnp.mean(d * d, axis=-1, keepdims=True)
    return d * lax.rsqrt(var + LN_EPS) * g + b


def _dot(a, b, precision=None):
    return jnp.dot(a, b, precision=precision, preferred_element_type=F32)


def _dot_nt(a, b, precision=None):
    return lax.dot_general(a, b, (((1,), (1,)), ((), ())), precision=precision,
                           preferred_element_type=F32)


KBAR_PAGES_PER_STEP = 8


def _ffn_ln_kernel(*refs, alpha, kbar_steps):
    n_pages = KBAR_PAGES_PER_STEP if kbar_steps else 0
    refs = refs[1:] if kbar_steps else refs
    x_ref, wg_ref, wu_ref, wd_ref, g_ref, b_ref = refs[:6]
    pages = refs[6:6 + n_pages]
    o_ref = refs[6 + n_pages]
    kbar_ref = refs[7 + n_pages] if kbar_steps else None
    xb_ref = refs[-1]
    i = pl.program_id(0)
    j = pl.program_id(1)

    @pl.when(j == 0)
    def _():
        xb_ref[...] = x_ref[...].astype(BF16)
        o_ref[...] = jnp.zeros_like(o_ref)

    xb = xb_ref[...]
    gate = _dot(xb, wg_ref[...].astype(BF16))
    up = _dot(xb, wu_ref[...].astype(BF16))
    hid = (gate * jax.nn.sigmoid(gate) * up).astype(BF16)
    o_ref[...] += _dot(hid, wd_ref[...].astype(BF16))

    if kbar_steps:
        @pl.when(i * pl.num_programs(1) + j < kbar_steps)
        def _():
            for blk in range(n_pages // PAGES_PER_BLOCK):
                tot = None
                for p in range(PAGES_PER_BLOCK):
                    part = jnp.sum(pages[blk * PAGES_PER_BLOCK + p][...], axis=0)
                    tot = part if tot is None else tot + part
                kbar_ref[blk] = tot * (1.0 / BLOCK)

    @pl.when(j == pl.num_programs(1) - 1)
    def _():
        o_ref[...] = _layer_norm(alpha * x_ref[...] + 0.5 * o_ref[...], g_ref[...], b_ref[...])


def _ffn_ln(x, wg, wu, wd, gain, bias, layer, alpha, tf=256, kbar_from=None):
    t, d = x.shape
    f = wg.shape[-1]
    tm = DENSE_ROW_TILE
    grid = (t // tm, f // tf)
    in_specs = [
        pl.BlockSpec((tm, d), lambda i, j, *_: (i, 0), pipeline_mode=pl.Buffered(1)),
        pl.BlockSpec((None, d, tf), lambda i, j, *_: (layer, 0, j)),
        pl.BlockSpec((None, d, tf), lambda i, j, *_: (layer, 0, j)),
        pl.BlockSpec((None, tf, d), lambda i, j, *_: (layer, j, 0)),
        pl.BlockSpec((1, d), lambda i, j, *_: (0, 0)),
        pl.BlockSpec((1, d), lambda i, j, *_: (0, 0)),
    ]
    out_specs = pl.BlockSpec((tm, d), lambda i, j, *_: (i, 0))
    out_shape = jax.ShapeDtypeStruct((t, d), F32)
    scratch = [pltpu.VMEM((tm, d), BF16)]
    if kbar_from is None:
        return pl.pallas_call(
            functools.partial(_ffn_ln_kernel, alpha=alpha, kbar_steps=0),
            grid=grid, in_specs=in_specs, out_specs=out_specs, out_shape=out_shape, scratch_shapes=scratch,
            compiler_params=_params("parallel", "arbitrary"), name="ffn_ln",
        )(x, wg, wu, wd, gain, bias)

    cache_k, pt_flat, dec_batch, n_pages = kbar_from
    pps = KBAR_PAGES_PER_STEP
    kbar_steps = dec_batch * n_pages // pps
    steps_per_seq = n_pages // pps
    blocks_per_step = pps // PAGES_PER_BLOCK
    assert kbar_steps <= grid[0] * grid[1]

    def side_step(i, j):
        return jnp.minimum(i * grid[1] + j, kbar_steps - 1)

    def page_spec(p):
        return pl.BlockSpec((None, None, PAGE_SIZE, N_HEADS, HEAD_DIM),
                            lambda i, j, pt: (layer, pt[side_step(i, j) * pps + p], 0, 0, 0))

    kbar_spec = pl.BlockSpec(
        (None, blocks_per_step, N_HEADS, HEAD_DIM),
        lambda i, j, pt: (side_step(i, j) // steps_per_seq, side_step(i, j) % steps_per_seq, 0, 0))
    out_specs = pl.BlockSpec((tm, d), lambda i, j, *_: (i, 0), pipeline_mode=pl.Buffered(1))
    return pl.pallas_call(
        functools.partial(_ffn_ln_kernel, alpha=alpha, kbar_steps=kbar_steps),
        grid_spec=pltpu.PrefetchScalarGridSpec(
            num_scalar_prefetch=1, grid=grid,
            in_specs=in_specs + [page_spec(p) for p in range(pps)],
            out_specs=[out_specs, kbar_spec], scratch_shapes=scratch),
        out_shape=[out_shape, jax.ShapeDtypeStruct((dec_batch, n_pages // PAGES_PER_BLOCK, N_HEADS, HEAD_DIM), F32)],
        compiler_params=_params("arbitrary", "arbitrary"), name="ffn_ln_kbar",
    )(pt_flat, x, wg, wu, wd, gain, bias, *([cache_k] * pps))


INPROJ_HEADS_PER_TILE = 4
INPROJ_TILES_PER_SECTION = N_HEADS // INPROJ_HEADS_PER_TILE


def _inproj_kernel(x_ref, w_ref, c_ref, s1_ref, s2_ref, o_ref, k3_ref, v3_ref, xb_ref):
    j = pl.program_id(1)
    tps = INPROJ_TILES_PER_SECTION

    @pl.when(j == 0)
    def _():
        xb_ref[...] = x_ref[...].astype(BF16)

    def heads():
        half = INPROJ_HEADS_PER_TILE // 2
        for part in range(2):
            cols = slice(part * half * HEAD_DIM, (part + 1) * half * HEAD_DIM)
            y = _dot(xb_ref[...], w_ref[:, cols].astype(BF16))
            for hl in range(half):
                yield part * half + hl, y[:, hl * HEAD_DIM:(hl + 1) * HEAD_DIM]

    def rope(yh):
        return (yh * c_ref[...]
                + pltpu.roll(yh, ROT_HALF, 1) * s1_ref[...]
                + pltpu.roll(yh, HEAD_DIM - ROT_HALF, 1) * s2_ref[...])

    @pl.when(j < tps)
    def _():
        for hh, yh in heads():
            o_ref[:, hh * HEAD_DIM:(hh + 1) * HEAD_DIM] = rope(yh)

    for tile in range(tps):
        head0 = tile * INPROJ_HEADS_PER_TILE

        @pl.when(j == tps + tile)
        def _():
            for hh, yh in heads():
                kh = rope(yh)
                o_ref[:, hh * HEAD_DIM:(hh + 1) * HEAD_DIM] = kh
                k3_ref[:, head0 + hh, :] = kh

        @pl.when(j == 2 * tps + tile)
        def _():
            for hh, yh in heads():
                o_ref[:, hh * HEAD_DIM:(hh + 1) * HEAD_DIM] = yh
                v3_ref[:, head0 + hh, :] = yh

    @pl.when(j >= 3 * tps)
    def _():
        o_ref[...] = _dot(xb_ref[...], w_ref[...].astype(BF16))


def _in_proj(x, w_in, rope_c, rope_s1, rope_s2, layer):
    t, d = x.shape
    n = w_in.shape[-1]
    tm, tn = DENSE_ROW_TILE, INPROJ_HEADS_PER_TILE * HEAD_DIM
    kv_spec = pl.BlockSpec((tm, N_HEADS, HEAD_DIM), lambda i, j: (i, 0, 0))
    kv_shape = jax.ShapeDtypeStruct((t, N_HEADS, HEAD_DIM), F32)
    return pl.pallas_call(
        _inproj_kernel,
        grid=(t // tm, n // tn),
        in_specs=[
            pl.BlockSpec((tm, d), lambda i, j: (i, 0), pipeline_mode=pl.Buffered(1)),
            pl.BlockSpec((None, d, tn), lambda i, j: (layer, 0, j)),
            pl.BlockSpec((tm, HEAD_DIM), lambda i, j: (i, 0)),
            pl.BlockSpec((tm, HEAD_DIM), lambda i, j: (i, 0)),
            pl.BlockSpec((tm, HEAD_DIM), lambda i, j: (i, 0)),
        ],
        out_specs=[pl.BlockSpec((tm, tn), lambda i, j: (i, j)), kv_spec, kv_spec],
        out_shape=[jax.ShapeDtypeStruct((t, n), F32), kv_shape, kv_shape],
        scratch_shapes=[pltpu.VMEM((tm, d), BF16)],
        compiler_params=_params("parallel", "arbitrary"),
        name="in_proj_rope",
    )(x, w_in, rope_c, rope_s1, rope_s2)


def _rope_tables(pos):
    inv = jnp.power(ROPE_THETA, -jnp.arange(ROT_HALF, dtype=F32) * 2.0 / ROT_DIM)
    ang = pos.astype(F32)[:, None] * inv[None, :]
    cos, sin = jnp.cos(ang), jnp.sin(ang)
    t = pos.shape[0]
    ones = jnp.ones((t, HEAD_DIM - ROT_DIM), F32)
    zeros_h = jnp.zeros((t, ROT_HALF), F32)
    zeros_r = jnp.zeros((t, HEAD_DIM - ROT_DIM), F32)
    c = jnp.concatenate([cos, cos, ones], axis=1)
    s1 = jnp.concatenate([zeros_h, sin, zeros_r], axis=1)
    s2 = jnp.concatenate([-sin, zeros_h, zeros_r], axis=1)
    return c, s1, s2


MOBA_HEADS_PER_STEP = 2


def _moba_prompt_kernel(q_ref, k_ref, v_ref, base_ref, o_ref, kb_ref, vt_ref, kbar_ref, *, n_blocks, scale):
    del base_ref
    c = pl.program_id(2)
    lanes_of = lambda hd: slice(hd * HEAD_DIM, (hd + 1) * HEAD_DIM)

    @pl.when(c == 0)
    def _():
        for n in range(n_blocks):
            kbar_ref[n:n + 1, :] = jnp.mean(k_ref[n * BLOCK:(n + 1) * BLOCK, :], axis=0, keepdims=True)
        kb_ref[...] = k_ref[...].astype(BF16)
        vt_ref[...] = v_ref[...].T.astype(BF16)

    blk_id = lax.broadcasted_iota(jnp.int32, (n_blocks, BLOCK), 0)
    key_row = lax.broadcasted_iota(jnp.int32, (BLOCK, BLOCK), 0)
    qry_col = lax.broadcasted_iota(jnp.int32, (BLOCK, BLOCK), 1)
    causal = key_row <= qry_col

    def pick_blocks(q, kbar):
        gates = _dot_nt(kbar, q, precision=HIGHEST)
        rank = jnp.zeros((n_blocks, BLOCK), F32)
        for m in range(n_blocks):
            gm = gates[m:m + 1, :]
            past = jnp.where(m < c, 1.0, 0.0)
            ge = jnp.where(gm >= gates, past, 0.0)
            gt = jnp.where(gm > gates, past, 0.0)
            rank = rank + jnp.where(blk_id > m, ge, gt)
        return jnp.where(rank < TOP_K, jnp.where(blk_id < c, 1.0, 0.0), 0.0)

    def attend_head(hd, sel, qb, nb, first_own):
        outs = []
        for qs in range(BLOCK // LANES):
            q_rows = slice(qs * LANES, (qs + 1) * LANES)
            qb_s = qb[q_rows, :]
            m = l = acc = None
            for n in range(nb):
                s = _dot_nt(kb_ref[n * BLOCK:(n + 1) * BLOCK, lanes_of(hd)], qb_s)
                keep = sel[n:n + 1, q_rows]
                if n >= first_own:
                    keep = keep + jnp.where(causal[:, q_rows], jnp.where(c == n, 1.0, 0.0), 0.0)
                s = jnp.where(keep > 0.5, s, NEG)
                bm = s.max(axis=0, keepdims=True)
                vt_n = vt_ref[lanes_of(hd), n * BLOCK:(n + 1) * BLOCK]
                if m is None:
                    m, p = bm, jnp.exp(s - bm)
                    l, acc = p.sum(axis=0, keepdims=True), _dot(vt_n, p.astype(BF16))
                else:
                    m_new = jnp.maximum(m, bm)
                    a, p = jnp.exp(m - m_new), jnp.exp(s - m_new)
                    l = a * l + p.sum(axis=0, keepdims=True)
                    acc = a * acc + _dot(vt_n, p.astype(BF16))
                    m = m_new
            outs.append((acc * (1.0 / l)).T)
        return jnp.concatenate(outs, axis=0).astype(o_ref.dtype)

    heads = range(MOBA_HEADS_PER_STEP)
    qs = [q_ref[:, lanes_of(hd)] for hd in heads]
    sels = [pick_blocks(qs[hd], kbar_ref[:, lanes_of(hd)]) for hd in heads]
    qbs = [(q * scale).astype(BF16) for q in qs]

    def attend(nb, first_own):
        o_ref[...] = jnp.concatenate([attend_head(hd, sels[hd], qbs[hd], nb, first_own) for hd in heads], axis=1)

    step = 2
    for nb in range(step, n_blocks + 1, step):
        pl.when(jnp.logical_and(c < nb, c >= nb - step))(functools.partial(attend, nb, nb - step))


def _moba_prompt(h, base, batch, seq):
    n_blocks = seq // BLOCK
    width = MOBA_HEADS_PER_STEP * HEAD_DIM
    groups = N_HEADS // MOBA_HEADS_PER_STEP
    return pl.pallas_call(
        functools.partial(_moba_prompt_kernel, n_blocks=n_blocks, scale=HEAD_DIM ** -0.5),
        grid=(batch, groups, n_blocks),
        in_specs=[
            pl.BlockSpec((BLOCK, width), lambda b, hg, c: (b * n_blocks + c, hg)),
            pl.BlockSpec((seq, width), lambda b, hg, c: (b, groups + hg)),
            pl.BlockSpec((seq, width), lambda b, hg, c: (b, 2 * groups + hg)),
            pl.BlockSpec(memory_space=pl.ANY),
        ],
        out_specs=pl.BlockSpec((BLOCK, width), lambda b, hg, c: (b * n_blocks + c, hg)),
        out_shape=jax.ShapeDtypeStruct(base.shape, base.dtype),
        input_output_aliases={3: 0},
        scratch_shapes=[pltpu.VMEM((seq, width), BF16), pltpu.VMEM((width, seq), BF16),
                        pltpu.VMEM((n_blocks, width), F32)],
        compiler_params=_params("parallel", "parallel", "arbitrary"),
        name="moba_prompt",
    )(h, h, h, base)


def _sample_topk_kernel(q_ref, kbar_ref, sel_ref):
    n_past = kbar_ref.shape[0]
    rows = q_ref.shape[0]
    lane_g = lax.broadcasted_iota(jnp.int32, (rows, n_past), 1).astype(F32)
    lane_o = lax.broadcasted_iota(jnp.int32, (rows, HEAD_DIM), 1)
    for hh in range(N_HEADS):
        qh = q_ref[:, hh * HEAD_DIM:(hh + 1) * HEAD_DIM]
        g = _dot_nt(qh, kbar_ref[:, hh, :], precision=HIGHEST)
        out = jnp.zeros((rows, HEAD_DIM), jnp.int32)
        for kk in range(TOP_K):
            best = jnp.max(g, axis=-1, keepdims=True)
            idx = jnp.min(jnp.where(g == best, lane_g, float(n_past)), axis=-1, keepdims=True)
            out = jnp.where(lane_o == kk, idx.astype(jnp.int32), out)
            g = jnp.where(lane_g == idx, -jnp.inf, g)
        sel_ref[hh] = out


def _sample_topk(q8, kbar):
    dec_batch, rows, width = q8.shape
    n_past = kbar.shape[1]
    return pl.pallas_call(
        _sample_topk_kernel,
        grid=(dec_batch,),
        in_specs=[
            pl.BlockSpec((None, rows, width), lambda b: (b, 0, 0)),
            pl.BlockSpec((None, n_past, N_HEADS, HEAD_DIM), lambda b: (b, 0, 0, 0)),
        ],
        out_specs=pl.BlockSpec((None, N_HEADS, rows, HEAD_DIM), lambda b: (b, 0, 0, 0)),
        out_shape=jax.ShapeDtypeStruct((dec_batch, N_HEADS, rows, HEAD_DIM), jnp.int32),
        compiler_params=_params("parallel"),
        name="sample_topk",
    )(q8, kbar)


def _sample_attn_kernel(pt_ref, sel_ref, q_ref, kn_ref, vn_ref, ck_hbm, cv_hbm, o_ref, kbuf, vbuf, sem,
                        *, layer, dec_seq, n_pages, scale):
    b = pl.program_id(0)
    hd = pl.program_id(1)
    step = b * N_HEADS + hd
    n_steps = pl.num_programs(0) * N_HEADS
    slot = step % 2
    sel_per_head = dec_seq * TOP_K
    n_blk = dec_seq * TOP_K

    def page_copies(bb, hh, sl):
        copies = []
        for i in range(n_blk):
            blk = sel_ref[(bb * N_HEADS + hh) * sel_per_head + i]
            for p in range(PAGES_PER_BLOCK):
                page = pt_ref[bb * n_pages + blk * PAGES_PER_BLOCK + p]
                dst = i * PAGES_PER_BLOCK + p
                copies.append(pltpu.make_async_copy(ck_hbm.at[layer, page, :, hh, :], kbuf.at[sl, dst], sem.at[0, sl]))
                copies.append(pltpu.make_async_copy(cv_hbm.at[layer, page, :, hh, :], vbuf.at[sl, dst], sem.at[1, sl]))
        return copies

    @pl.when(step == 0)
    def _():
        for cp in page_copies(b, hd, slot):
            cp.start()

    @pl.when(step + 1 < n_steps)
    def _():
        nxt = step + 1
        for cp in page_copies(nxt // N_HEADS, nxt % N_HEADS, 1 - slot):
            cp.start()

    for cp in page_copies(b, hd, slot):
        cp.wait()

    rows = q_ref.shape[0]
    qb = q_ref[...].astype(BF16)
    s_own_all = _dot_nt(qb, kn_ref[...].astype(BF16)) * scale
    vn = vn_ref[...].astype(BF16)
    col = lax.broadcasted_iota(jnp.int32, (1, rows), 1)

    def block(buf, i):
        return jnp.concatenate([buf[slot, i * PAGES_PER_BLOCK + p] for p in range(PAGES_PER_BLOCK)],
                               axis=0).astype(BF16)

    s_sel = [[_dot_nt(qb, block(kbuf, qi * TOP_K + kk))[qi:qi + 1, :] * scale for kk in range(TOP_K)]
             for qi in range(dec_seq)]
    p_own, p_sel, denoms = [], [], []
    for qi in range(dec_seq):
        s_own = jnp.where(col <= qi, s_own_all[qi:qi + 1, :], NEG)
        m = jnp.max(s_own, axis=-1, keepdims=True)
        for s in s_sel[qi]:
            m = jnp.maximum(m, jnp.max(s, axis=-1, keepdims=True))
        po = jnp.exp(s_own - m)
        ps = [jnp.exp(s - m) for s in s_sel[qi]]
        denom = jnp.sum(po, axis=-1, keepdims=True)
        for p in ps:
            denom = denom + jnp.sum(p, axis=-1, keepdims=True)
        p_own.append(po)
        p_sel.append(ps)
        denoms.append(denom)
    row = lax.broadcasted_iota(jnp.int32, (rows, HEAD_DIM), 0)
    result = jnp.zeros((rows, HEAD_DIM), F32)
    for qi in range(dec_seq):
        out = _dot(jnp.broadcast_to(p_own[qi], (rows, rows)).astype(BF16), vn)
        for kk in range(TOP_K):
            p = jnp.broadcast_to(p_sel[qi][kk], (rows, BLOCK)).astype(BF16)
            out = out + _dot(p, block(vbuf, qi * TOP_K + kk))
        result = jnp.where(row == qi, out / denoms[qi], result)
    o_ref[...] = result


def _sample_attn(q8, k8, v8, cache_k, cache_v, pt_flat, sel_flat, layer, dec_seq, n_pages):
    dec_batch, rows, _ = q8.shape
    n_sel_pages = dec_seq * TOP_K * PAGES_PER_BLOCK
    head_spec = pl.BlockSpec((None, rows, HEAD_DIM), lambda b, hh, pt, sel: (b, 0, hh))
    hbm = pl.BlockSpec(memory_space=pl.ANY)
    return pl.pallas_call(
        functools.partial(_sample_attn_kernel, layer=layer, dec_seq=dec_seq, n_pages=n_pages,
                          scale=HEAD_DIM ** -0.5),
        grid_spec=pltpu.PrefetchScalarGridSpec(
            num_scalar_prefetch=2,
            grid=(dec_batch, N_HEADS),
            in_specs=[head_spec, head_spec, head_spec, hbm, hbm],
            out_specs=head_spec,
            scratch_shapes=[pltpu.VMEM((2, n_sel_pages, PAGE_SIZE, HEAD_DIM), F32),
                            pltpu.VMEM((2, n_sel_pages, PAGE_SIZE, HEAD_DIM), F32),
                            pltpu.SemaphoreType.DMA((2, 2))],
        ),
        out_shape=jax.ShapeDtypeStruct(q8.shape, F32),
        compiler_params=_params("arbitrary", "arbitrary"),
        name="sample_attn",
    )(pt_flat, sel_flat, q8, k8, v8, cache_k, cache_v)


def _s5_tables(a_re, a_im, log_dt, b_re, b_im, c_re, c_im):
    L = SSM_CHUNK
    lam = lax.complex(a_re, a_im)
    dt = jnp.exp(log_dt)[..., None]
    lam_bar = jnp.exp(lam * dt)
    b_bar = ((lam_bar - 1.0) / lam)[..., None] * lax.complex(b_re, b_im)
    c_mat = lax.complex(c_re, c_im)
    pw = [jnp.ones_like(lam_bar)]
    for _ in range(L):
        pw.append(pw[-1] * lam_bar)
    pows = jnp.stack(pw, axis=2)
    z = (c_mat.transpose(0, 1, 3, 2)[:, :, :, None, :]
         * pows.transpose(0, 1, 3, 2)[:, :, :, :, None])
    dd, gg = z.shape[0], z.shape[1]
    z = z.reshape(dd, gg, SSM_STATE, (L + 1) * SSM_GROUP)
    zt = jnp.concatenate([jnp.real(z), -jnp.imag(z)], axis=2)
    bst = jnp.concatenate([jnp.real(b_bar), jnp.imag(b_bar)], axis=2)
    gw = L * SSM_GROUP
    r0 = jnp.einsum('dgpi,dgpn->dgin', bst, zt[..., :gw], precision=HIGHEST)
    tt = jnp.stack([jnp.pad(r0[..., :gw - t * SSM_GROUP], ((0, 0), (0, 0), (0, 0), (t * SSM_GROUP, 0)))
                    for t in range(L)], axis=2).reshape(dd, gg, gw, gw)
    nt = zt[..., SSM_GROUP:]
    mcx = (pows[:, :, L - 1::-1][:, :, :, None, :]
           * b_bar.transpose(0, 1, 3, 2)[:, :, None])
    mcx = mcx.reshape(dd, gg, gw, SSM_STATE)
    mt = jnp.concatenate([jnp.real(mcx), jnp.imag(mcx)], axis=-1)

    def lanes(zc):
        return (jnp.concatenate([jnp.real(zc), jnp.real(zc)], axis=-1),
                jnp.concatenate([-jnp.imag(zc), jnp.imag(zc)], axis=-1))

    steps = []
    cur = pows[:, :, L]
    for _ in range(8):
        steps.append(cur)
        cur = cur * cur
    ar, ai = lanes(jnp.stack(steps, axis=2))
    return dict(tt=tt, mt=mt, nt=nt, ar=ar, ai=ai, pows=pows)


def _chunk_permutation():
    n = SSM_CHUNK * LANES
    src = jnp.arange(n)
    t, g, ch = src // LANES, (src % LANES) // SSM_GROUP, src % SSM_GROUP
    dst = g * (SSM_CHUNK * SSM_GROUP) + t * SSM_GROUP + ch
    return (dst[:, None] == jnp.arange(n)[None, :]).astype(BF16)


def _s5_prompt_kernel(x_ref, perm_ref, tt_ref, mt_ref, nt_ref, ar_ref, ai_ref, d_ref, base_ref, y_ref, hl_ref,
                      *, n_seq, n_chunks):
    del base_ref
    L = SSM_CHUNK
    gw = L * SSM_GROUP
    rows = n_seq * n_chunks
    u_tok = jnp.concatenate([x_ref[:, t, :].astype(BF16) for t in range(L)], axis=1)
    u_grp = _dot(u_tok, perm_ref[...]).astype(BF16)
    kpos = lax.broadcasted_iota(jnp.int32, (rows, 2 * SSM_STATE), 0) % n_chunks
    y_parts = []
    for g in range(GROUPS_PER_LANE_BLOCK):
        ub = u_grp[:, g * gw:(g + 1) * gw]
        w = _dot(ub, mt_ref[g])
        stride, i = 1, 0
        while stride < n_chunks:
            sh = jnp.where(kpos >= stride, pltpu.roll(w, stride, 0), 0.0)
            w = w + sh * ar_ref[g, i:i + 1, :] + pltpu.roll(sh, SSM_STATE, 1) * ai_ref[g, i:i + 1, :]
            stride, i = stride * 2, i + 1
        h_in = jnp.where(kpos >= 1, pltpu.roll(w, 1, 0), 0.0)
        y_parts.append((_dot(ub, tt_ref[g]) + _dot(h_in.astype(BF16), nt_ref[g])).astype(BF16))
        for n in range(n_seq):
            hl_ref[g, n:n + 1, :] = w[(n + 1) * n_chunks - 1:(n + 1) * n_chunks, :]
    y_tok = _dot_nt(jnp.concatenate(y_parts, axis=1), perm_ref[...])
    for t in range(L):
        y_ref[:, t, :] = y_tok[:, t * LANES:(t + 1) * LANES] + d_ref[...] * x_ref[:, t, :]


def _s5_prompt(h3, base3, perm, tt, mt, nt, ar, ai, d_lane, layer, n_seq_total, n_chunks, u_lane_block0,
               seq_per_step=2):
    groups = tt.shape[1]
    lane_blocks = groups // GROUPS_PER_LANE_BLOCK
    rows = seq_per_step * n_chunks
    p2 = 2 * SSM_STATE
    gw = SSM_CHUNK * SSM_GROUP
    gpb = GROUPS_PER_LANE_BLOCK
    n_perm = SSM_CHUNK * LANES
    return pl.pallas_call(
        functools.partial(_s5_prompt_kernel, n_seq=seq_per_step, n_chunks=n_chunks),
        grid=(n_seq_total // seq_per_step, lane_blocks),
        in_specs=[
            pl.BlockSpec((rows, SSM_CHUNK, LANES), lambda s, g: (s, 0, u_lane_block0 + g)),
            pl.BlockSpec((n_perm, n_perm), lambda s, g: (0, 0)),
            pl.BlockSpec((None, gpb, gw, gw), lambda s, g: (layer, g, 0, 0)),
            pl.BlockSpec((None, gpb, gw, p2), lambda s, g: (layer, g, 0, 0)),
            pl.BlockSpec((None, gpb, p2, gw), lambda s, g: (layer, g, 0, 0)),
            pl.BlockSpec((None, gpb, 8, p2), lambda s, g: (layer, g, 0, 0)),
            pl.BlockSpec((None, gpb, 8, p2), lambda s, g: (layer, g, 0, 0)),
            pl.BlockSpec((None, None, 1, LANES), lambda s, g: (layer, g, 0, 0)),
            pl.BlockSpec(memory_space=pl.ANY),
        ],
        out_specs=[
            pl.BlockSpec((rows, SSM_CHUNK, LANES), lambda s, g: (s, 0, g)),
            pl.BlockSpec((None, gpb, seq_per_step, p2), lambda s, g: (s, g, 0, 0)),
        ],
        out_shape=[
            jax.ShapeDtypeStruct(base3.shape, base3.dtype),
            jax.ShapeDtypeStruct((n_seq_total // seq_per_step, groups, seq_per_step, p2), F32),
        ],
        input_output_aliases={8: 0},
        compiler_params=_params("parallel", "arbitrary"),
        name="s5_prompt",
    )(h3, perm, tt, mt, nt, ar, ai, d_lane, base3)


def _s5_sample_kernel(u_ref, h0_ref, tt_ref, mt_ref, nt_ref, ar_ref, ai_ref, d_ref, y_ref, hn_ref):
    groups = u_ref.shape[0]

    def body(g, carry):
        u = u_ref[g]
        h0 = h0_ref[g]
        y_ref[g] = _dot(u, tt_ref[g], HIGHEST) + _dot(h0, nt_ref[g], HIGHEST) + d_ref[g] * u
        hn_ref[g] = (h0 * ar_ref[g] + pltpu.roll(h0, SSM_STATE, 1) * ai_ref[g]
                     + _dot(u, mt_ref[g], HIGHEST))
        return carry

    lax.fori_loop(0, groups, body, 0)


def _s5_sample(u_g, h0_g, tt_s, mt_s, nt_s, ar_s, ai_s, d_s):
    args = (u_g, h0_g, tt_s, mt_s, nt_s, ar_s, ai_s, d_s)
    full = lambda a: pl.BlockSpec(a.shape, lambda i, nd=a.ndim: (0,) * nd)
    return pl.pallas_call(
        _s5_sample_kernel,
        grid=(1,),
        in_specs=[full(a) for a in args],
        out_specs=[full(u_g), full(h0_g)],
        out_shape=[jax.ShapeDtypeStruct(u_g.shape, F32), jax.ShapeDtypeStruct(h0_g.shape, F32)],
        compiler_params=_params("arbitrary"),
        name="s5_sample",
    )(*args)


def _mix_ln_kernel(a_ref, ys_ref, ga_ref, gb_ref, x_ref, wa_ref, wv_ref, wg_ref, wo_ref, g_ref, b_ref,
                   o_ref, zb_ref, acc_ref, *, alpha):
    j = pl.program_id(1)

    @pl.when(j == 0)
    def _():
        zb_ref[...] = jax.nn.gelu(ys_ref[...]).astype(BF16)
        acc_ref[...] = jnp.zeros_like(acc_ref)

    zb = zb_ref[...]
    y_a = _dot(a_ref[...], wa_ref[...])
    y_b = _dot(zb, wv_ref[...]) * jax.nn.sigmoid(_dot(zb, wg_ref[...]))
    mixed = jax.nn.sigmoid(ga_ref[...]) * y_a + jax.nn.sigmoid(gb_ref[...]) * y_b
    acc_ref[...] += _dot(mixed.astype(BF16), wo_ref[...])

    @pl.when(j == pl.num_programs(1) - 1)
    def _():
        o_ref[...] = _layer_norm(alpha * x_ref[...] + acc_ref[...], g_ref[...], b_ref[...])


def _mix_ln(attn, y_ssm, h, x, wa, wv, wg, wo, gain, bias, layer, alpha, tn=512):
    t, d = x.shape
    aw = attn.shape[1]
    sw = y_ssm.shape[1]
    tm = ROW_TILE
    ga_blk = (3 * ATTN_WIDTH + sw) // tn
    gb_blk = ga_blk + d // tn
    return pl.pallas_call(
        functools.partial(_mix_ln_kernel, alpha=alpha),
        grid=(t // tm, d // tn),
        in_specs=[
            pl.BlockSpec((tm, aw), lambda i, j: (i, 0)),
            pl.BlockSpec((tm, sw), lambda i, j: (i, 0)),
            pl.BlockSpec((tm, tn), lambda i, j: (i, ga_blk + j)),
            pl.BlockSpec((tm, tn), lambda i, j: (i, gb_blk + j)),
            pl.BlockSpec((tm, d), lambda i, j: (i, 0)),
            pl.BlockSpec((None, aw, tn), lambda i, j: (layer, 0, j)),
            pl.BlockSpec((None, sw, tn), lambda i, j: (layer, 0, j)),
            pl.BlockSpec((None, sw, tn), lambda i, j: (layer, 0, j)),
            pl.BlockSpec((None, tn, d), lambda i, j: (layer, j, 0)),
            pl.BlockSpec((1, d), lambda i, j: (0, 0)),
            pl.BlockSpec((1, d), lambda i, j: (0, 0)),
        ],
        out_specs=pl.BlockSpec((tm, d), lambda i, j: (i, 0)),
        out_shape=jax.ShapeDtypeStruct((t, d), F32),
        scratch_shapes=[pltpu.VMEM((tm, sw), BF16), pltpu.VMEM((tm, d), F32)],
        compiler_params=_params("parallel", "arbitrary"),
        name="mix_ln",
    )(attn, y_ssm, h, h, x, wa, wv, wg, wo, gain, bias)


def kernel(x_prompt, x_sample, cache_k, cache_v, state_ssm_re, state_ssm_im, page_table, w_in, w_attn_out, w_glu_v, w_glu_g, w_out, ssm_a_re, ssm_a_im, ssm_log_dt, ssm_b_re, ssm_b_im, ssm_c_re, ssm_c_im, ssm_d, ffn1_gate, ffn1_up, ffn1_down, ffn2_gate, ffn2_up, ffn2_down, ln_gain, ln_bias):
    batch, seq, d_model = x_prompt.shape
    dec_batch, dec_seq, _ = x_sample.shape
    depth = w_in.shape[0]
    n_pages = page_table.shape[1]
    past_len = n_pages * PAGE_SIZE
    ssm_width = ssm_d.shape[1]
    groups = ssm_width // SSM_GROUP
    L, CH = SSM_CHUNK, SSM_GROUP
    n_chunks = seq // L
    t_p = batch * seq
    t_s = dec_batch * dec_seq
    t_pad = -(-(t_p + t_s) // DENSE_ROW_TILE) * DENSE_ROW_TILE
    t_tail = t_pad - t_p
    assert past_len % BLOCK == 0 and past_len // BLOCK >= TOP_K
    assert seq % (2 * BLOCK) == 0 and t_pad % L == 0 and batch % 2 == 0
    assert dec_seq <= 8 and dec_seq <= L and n_pages % KBAR_PAGES_PER_STEP == 0
    assert (3 * ATTN_WIDTH) % LANES == 0 and groups % GROUPS_PER_LANE_BLOCK == 0
    alpha = (2.0 * depth) ** 0.25

    x = jnp.concatenate([x_prompt.reshape(t_p, d_model), x_sample.reshape(t_s, d_model),
                         jnp.zeros((t_tail - t_s, d_model), F32)], axis=0)

    pos = jnp.concatenate([jnp.tile(jnp.arange(seq, dtype=jnp.int32), batch),
                           jnp.tile(past_len + jnp.arange(dec_seq, dtype=jnp.int32), dec_batch),
                           jnp.zeros((t_tail - t_s,), jnp.int32)])
    rope_c, rope_s1, rope_s2 = _rope_tables(pos)

    w_ao_b, w_gv_b, w_gg_b, w_out_b = (w.astype(BF16) for w in (w_attn_out, w_glu_v, w_glu_g, w_out))

    tabs = _s5_tables(ssm_a_re, ssm_a_im, ssm_log_dt, ssm_b_re, ssm_b_im, ssm_c_re, ssm_c_im)
    tt_b, mt_b, nt_b = tabs['tt'].astype(BF16), tabs['mt'].astype(BF16), tabs['nt'].astype(BF16)
    perm = _chunk_permutation()
    d_lane = ssm_d.reshape(depth, ssm_width // LANES, 1, LANES)
    d_g = ssm_d.reshape(depth, groups, 1, CH)
    ws = dec_seq * CH
    tt_s = tabs['tt'][:, :, :ws, :ws]
    nt_s = tabs['nt'][:, :, :, :ws]
    mt_s = tabs['mt'][:, :, (L - dec_seq) * CH:, :]
    lam_ds = tabs['pows'][:, :, dec_seq]
    ar_s = jnp.concatenate([jnp.real(lam_ds), jnp.real(lam_ds)], axis=-1)[:, :, None, :]
    ai_s = jnp.concatenate([-jnp.imag(lam_ds), jnp.imag(lam_ds)], axis=-1)[:, :, None, :]
    d_s = jnp.tile(d_g, (1, 1, 1, dec_seq))
    h0_s = jnp.concatenate([state_ssm_re, state_ssm_im], axis=-1).transpose(0, 2, 1, 3)

    pt_flat = page_table.reshape(-1)
    row_pad = 8 - dec_seq

    def pad_rows(a):
        return jnp.pad(a, ((0, 0), (0, row_pad), (0, 0)))

    def tail_base(tail_rows):
        width, dtype = tail_rows.shape[1], tail_rows.dtype
        return jnp.concatenate([jnp.zeros((t_p, width), dtype), tail_rows,
                                jnp.zeros((t_tail - t_s, width), dtype)], axis=0)

    u0 = 3 * ATTN_WIDTH
    kp_l, vp_l, hrp_l, hip_l, ks_l, vs_l, hrs_l, his_l = ([] for _ in range(8))
    for l in range(depth):
        gains = [ln_gain[l, i][None, :] for i in range(3)]
        biases = [ln_bias[l, i][None, :] for i in range(3)]
        x, kbar = _ffn_ln(x, ffn1_gate, ffn1_up, ffn1_down, gains[0], biases[0], l, alpha,
                          kbar_from=(cache_k, pt_flat, dec_batch, n_pages))
        h, k3, v3 = _in_proj(x, w_in, rope_c, rope_s1, rope_s2, l)

        hs = h[t_p:t_p + t_s]
        q_s = hs[:, :ATTN_WIDTH].reshape(dec_batch, dec_seq, ATTN_WIDTH)
        k_s = hs[:, ATTN_WIDTH:2 * ATTN_WIDTH].reshape(dec_batch, dec_seq, ATTN_WIDTH)
        v_s = hs[:, 2 * ATTN_WIDTH:3 * ATTN_WIDTH].reshape(dec_batch, dec_seq, ATTN_WIDTH)
        q8, k8, v8 = pad_rows(q_s), pad_rows(k_s), pad_rows(v_s)
        sel = _sample_topk(q8, kbar)[:, :, :dec_seq, :TOP_K]
        attn_s = _sample_attn(q8, k8, v8, cache_k, cache_v, pt_flat, sel.reshape(-1), l, dec_seq, n_pages)
        attn_s = attn_s[:, :dec_seq].reshape(t_s, ATTN_WIDTH)
        attn = _moba_prompt(h, tail_base(attn_s.astype(BF16)), batch, seq)

        us_g = hs[:, u0:u0 + ssm_width].reshape(dec_batch, dec_seq, groups, CH)
        us_g = us_g.transpose(2, 0, 1, 3).reshape(groups, dec_batch, ws)
        ys_g, hn_g = _s5_sample(us_g, h0_s[l], tt_s[l], mt_s[l], nt_s[l], ar_s[l], ai_s[l], d_s[l])
        y_s = ys_g.reshape(groups, dec_batch, dec_seq, CH).transpose(1, 2, 0, 3).reshape(t_s, ssm_width)
        h3 = h.reshape(t_pad // L, L, h.shape[1])
        y3, hl4 = _s5_prompt(h3, tail_base(y_s).reshape(t_pad // L, L, ssm_width), perm, tt_b, mt_b, nt_b,
                             tabs['ar'], tabs['ai'], d_lane, l, batch, n_chunks, u0 // LANES)
        y_ssm = y3.reshape(t_pad, ssm_width)

        x = _mix_ln(attn, y_ssm, h, x, w_ao_b, w_gv_b, w_gg_b, w_out_b, gains[1], biases[1], l, alpha)
        x = _ffn_ln(x, ffn2_gate, ffn2_up, ffn2_down, gains[2], biases[2], l, alpha)

        kp_l.append(k3[:t_p].reshape(batch, seq, N_HEADS, HEAD_DIM))
        vp_l.append(v3[:t_p].reshape(batch, seq, N_HEADS, HEAD_DIM))
        hl = hl4.transpose(0, 2, 1, 3).reshape(batch, groups, 2 * SSM_STATE)
        hrp_l.append(hl[..., :SSM_STATE])
        hip_l.append(hl[..., SSM_STATE:])
        ks_l.append(k3[t_p:t_p + t_s].reshape(dec_batch, dec_seq, N_HEADS, HEAD_DIM))
        vs_l.append(v3[t_p:t_p + t_s].reshape(dec_batch, dec_seq, N_HEADS, HEAD_DIM))
        hn = hn_g.transpose(1, 0, 2)
        hrs_l.append(hn[..., :SSM_STATE])
        his_l.append(hn[..., SSM_STATE:])

    y_prompt = x[:t_p].reshape(batch, seq, d_model)
    y_sample = x[t_p:t_p + t_s].reshape(dec_batch, dec_seq, d_model)
    return (y_prompt, y_sample,
            jnp.stack(kp_l), jnp.stack(vp_l), jnp.stack(hrp_l), jnp.stack(hip_l),
            jnp.stack(ks_l), jnp.stack(vs_l), jnp.stack(hrs_l), jnp.stack(his_l))
```

```python
import functools

import jax
import jax.numpy as jnp
from jax import lax
from jax.experimental import pallas as pl
from jax.experimental.pallas import tpu as pltpu

N_HEADS = 8
HEAD_DIM = 128
ATTN_WIDTH = N_HEADS * HEAD_DIM
ROT_DIM = HEAD_DIM // 4
ROT_HALF = ROT_DIM // 2
ROPE_THETA = 500000.0
BLOCK = 256
TOP_K = 3
PAGE_SIZE = 128
PAGES_PER_BLOCK = BLOCK // PAGE_SIZE
SSM_GROUP = 16
SSM_STATE = 64
SSM_CHUNK = 16
LANES = 128
GROUPS_PER_LANE_BLOCK = LANES // SSM_GROUP
LN_EPS = 1e-5
NEG = -1e30

ROW_TILE = 528
DENSE_ROW_TILE = 2 * ROW_TILE
VMEM_LIMIT = 56 * 1024 * 1024

F32 = jnp.float32
BF16 = jnp.bfloat16
HIGHEST = lax.Precision.HIGHEST


def _params(*sem):
    return pltpu.CompilerParams(dimension_semantics=sem, vmem_limit_bytes=VMEM_LIMIT)


def _layer_norm(r, g, b):
    mu = jnp.mean(r, axis=-1, keepdims=True)
    d = r - mu
    var = jnp.mean(d * d, axis=-1, keepdims=True)
    return d * lax.rsqrt(var + LN_EPS) * g + b


def _dot(a, b, precision=None):
    return jnp.dot(a, b, precision=precision, preferred_element_type=F32)


def _dot_nt(a, b, precision=None):
    return lax.dot_general(a, b, (((1,), (1,)), ((), ())), precision=precision,
                           preferred_element_type=F32)


KBAR_PAGES_PER_STEP = 8


def _ffn_ln_kernel(*refs, alpha, kbar_steps):
    n_pages = KBAR_PAGES_PER_STEP if kbar_steps else 0
    refs = refs[1:] if kbar_steps else refs
    x_ref, wg_ref, wu_ref, wd_ref, g_ref, b_ref = refs[:6]
    pages = refs[6:6 + n_pages]
    o_ref = refs[6 + n_pages]
    kbar_ref = refs[7 + n_pages] if kbar_steps else None
    xb_ref = refs[-1]
    i = pl.program_id(0)
    j = pl.program_id(1)

    @pl.when(j == 0)
    def _():
        xb_ref[...] = x_ref[...].astype(BF16)
        o_ref[...] = jnp.zeros_like(o_ref)

    xb = xb_ref[...]
    gate = _dot(xb, wg_ref[...])
    up = _dot(xb, wu_ref[...])
    hid = (gate * jax.nn.sigmoid(gate) * up).astype(BF16)
    o_ref[...] += _dot(hid, wd_ref[...])

    if kbar_steps:
        @pl.when(i * pl.num_programs(1) + j < kbar_steps)
        def _():
            for blk in range(n_pages // PAGES_PER_BLOCK):
                tot = None
                for p in range(PAGES_PER_BLOCK):
                    part = jnp.sum(pages[blk * PAGES_PER_BLOCK + p][...], axis=0)
                    tot = part if tot is None else tot + part
                kbar_ref[blk] = tot * (1.0 / BLOCK)

    @pl.when(j == pl.num_programs(1) - 1)
    def _():
        o_ref[...] = _layer_norm(alpha * x_ref[...] + 0.5 * o_ref[...], g_ref[...], b_ref[...])


def _ffn_ln(x, wg, wu, wd, gain, bias, layer, alpha, tf=256, kbar_from=None):
    t, d = x.shape
    f = wg.shape[-1]
    tm = DENSE_ROW_TILE
    grid = (t // tm, f // tf)
    in_specs = [
        pl.BlockSpec((tm, d), lambda i, j, *_: (i, 0), pipeline_mode=pl.Buffered(1)),
        pl.BlockSpec((None, d, tf), lambda i, j, *_: (layer, 0, j)),
        pl.BlockSpec((None, d, tf), lambda i, j, *_: (layer, 0, j)),
        pl.BlockSpec((None, tf, d), lambda i, j, *_: (layer, j, 0)),
        pl.BlockSpec((1, d), lambda i, j, *_: (0, 0)),
        pl.BlockSpec((1, d), lambda i, j, *_: (0, 0)),
    ]
    out_specs = pl.BlockSpec((tm, d), lambda i, j, *_: (i, 0))
    out_shape = jax.ShapeDtypeStruct((t, d), F32)
    scratch = [pltpu.VMEM((tm, d), BF16)]
    if kbar_from is None:
        return pl.pallas_call(
            functools.partial(_ffn_ln_kernel, alpha=alpha, kbar_steps=0),
            grid=grid, in_specs=in_specs, out_specs=out_specs, out_shape=out_shape, scratch_shapes=scratch,
            compiler_params=_params("parallel", "arbitrary"), name="ffn_ln",
        )(x, wg, wu, wd, gain, bias)

    cache_k, pt_flat, dec_batch, n_pages = kbar_from
    pps = KBAR_PAGES_PER_STEP
    kbar_steps = dec_batch * n_pages // pps
    steps_per_seq = n_pages // pps
    blocks_per_step = pps // PAGES_PER_BLOCK
    assert kbar_steps <= grid[0] * grid[1]

    def side_step(i, j):
        return jnp.minimum(i * grid[1] + j, kbar_steps - 1)

    def page_spec(p):
        return pl.BlockSpec((None, None, PAGE_SIZE, N_HEADS, HEAD_DIM),
                            lambda i, j, pt: (layer, pt[side_step(i, j) * pps + p], 0, 0, 0))

    kbar_spec = pl.BlockSpec(
        (None, blocks_per_step, N_HEADS, HEAD_DIM),
        lambda i, j, pt: (side_step(i, j) // steps_per_seq, side_step(i, j) % steps_per_seq, 0, 0))
    return pl.pallas_call(
        functools.partial(_ffn_ln_kernel, alpha=alpha, kbar_steps=kbar_steps),
        grid_spec=pltpu.PrefetchScalarGridSpec(
            num_scalar_prefetch=1, grid=grid,
            in_specs=in_specs + [page_spec(p) for p in range(pps)],
            out_specs=[out_specs, kbar_spec], scratch_shapes=scratch),
        out_shape=[out_shape, jax.ShapeDtypeStruct((dec_batch, n_pages // PAGES_PER_BLOCK, N_HEADS, HEAD_DIM), F32)],
        compiler_params=_params("arbitrary", "arbitrary"), name="ffn_ln_kbar",
    )(pt_flat, x, wg, wu, wd, gain, bias, *([cache_k] * pps))


INPROJ_HEADS_PER_TILE = 4
INPROJ_TILES_PER_SECTION = N_HEADS // INPROJ_HEADS_PER_TILE


def _inproj_kernel(x_ref, w_ref, c_ref, s1_ref, s2_ref, o_ref, k3_ref, v3_ref, xb_ref):
    j = pl.program_id(1)
    tps = INPROJ_TILES_PER_SECTION

    @pl.when(j == 0)
    def _():
        xb_ref[...] = x_ref[...].astype(BF16)

    def heads():
        half = INPROJ_HEADS_PER_TILE // 2
        for part in range(2):
            cols = slice(part * half * HEAD_DIM, (part + 1) * half * HEAD_DIM)
            y = _dot(xb_ref[...], w_ref[:, cols])
            for hl in range(half):
                yield part * half + hl, y[:, hl * HEAD_DIM:(hl + 1) * HEAD_DIM]

    def rope(yh):
        return (yh * c_ref[...]
                + pltpu.roll(yh, ROT_HALF, 1) * s1_ref[...]
                + pltpu.roll(yh, HEAD_DIM - ROT_HALF, 1) * s2_ref[...])

    @pl.when(j < tps)
    def _():
        for hh, yh in heads():
            o_ref[:, hh * HEAD_DIM:(hh + 1) * HEAD_DIM] = rope(yh)

    for tile in range(tps):
        head0 = tile * INPROJ_HEADS_PER_TILE

        @pl.when(j == tps + tile)
        def _():
            for hh, yh in heads():
                kh = rope(yh)
                o_ref[:, hh * HEAD_DIM:(hh + 1) * HEAD_DIM] = kh
                k3_ref[:, head0 + hh, :] = kh

        @pl.when(j == 2 * tps + tile)
        def _():
            for hh, yh in heads():
                o_ref[:, hh * HEAD_DIM:(hh + 1) * HEAD_DIM] = yh
                v3_ref[:, head0 + hh, :] = yh

    @pl.when(j >= 3 * tps)
    def _():
        o_ref[...] = _dot(xb_ref[...], w_ref[...])


def _in_proj(x, w_in, rope_c, rope_s1, rope_s2, layer):
    t, d = x.shape
    n = w_in.shape[-1]
    tm, tn = DENSE_ROW_TILE, INPROJ_HEADS_PER_TILE * HEAD_DIM
    kv_spec = pl.BlockSpec((tm, N_HEADS, HEAD_DIM), lambda i, j: (i, 0, 0))
    kv_shape = jax.ShapeDtypeStruct((t, N_HEADS, HEAD_DIM), F32)
    return pl.pallas_call(
        _inproj_kernel,
        grid=(t // tm, n // tn),
        in_specs=[
            pl.BlockSpec((tm, d), lambda i, j: (i, 0), pipeline_mode=pl.Buffered(1)),
            pl.BlockSpec((None, d, tn), lambda i, j: (layer, 0, j)),
            pl.BlockSpec((tm, HEAD_DIM), lambda i, j: (i, 0)),
            pl.BlockSpec((tm, HEAD_DIM), lambda i, j: (i, 0)),
            pl.BlockSpec((tm, HEAD_DIM), lambda i, j: (i, 0)),
        ],
        out_specs=[pl.BlockSpec((tm, tn), lambda i, j: (i, j)), kv_spec, kv_spec],
        out_shape=[jax.ShapeDtypeStruct((t, n), F32), kv_shape, kv_shape],
        scratch_shapes=[pltpu.VMEM((tm, d), BF16)],
        compiler_params=_params("parallel", "arbitrary"),
        name="in_proj_rope",
    )(x, w_in, rope_c, rope_s1, rope_s2)


def _rope_tables(pos):
    inv = jnp.power(ROPE_THETA, -jnp.arange(ROT_HALF, dtype=F32) * 2.0 / ROT_DIM)
    ang = pos.astype(F32)[:, None] * inv[None, :]
    cos, sin = jnp.cos(ang), jnp.sin(ang)
    t = pos.shape[0]
    ones = jnp.ones((t, HEAD_DIM - ROT_DIM), F32)
    zeros_h = jnp.zeros((t, ROT_HALF), F32)
    zeros_r = jnp.zeros((t, HEAD_DIM - ROT_DIM), F32)
    c = jnp.concatenate([cos, cos, ones], axis=1)
    s1 = jnp.concatenate([zeros_h, sin, zeros_r], axis=1)
    s2 = jnp.concatenate([-sin, zeros_h, zeros_r], axis=1)
    return c, s1, s2


MOBA_HEADS_PER_STEP = 4


def _moba_prompt_kernel(q_ref, k_ref, v_ref, base_ref, o_ref, kb_ref, vt_ref, kbar_ref, *, n_blocks, scale):
    del base_ref
    c = pl.program_id(2)
    lanes_of = lambda hd: slice(hd * HEAD_DIM, (hd + 1) * HEAD_DIM)

    @pl.when(c == 0)
    def _():
        for n in range(n_blocks):
            kbar_ref[n:n + 1, :] = jnp.mean(k_ref[n * BLOCK:(n + 1) * BLOCK, :], axis=0, keepdims=True)
        kb_ref[...] = k_ref[...].astype(BF16)
        vt_ref[...] = v_ref[...].T.astype(BF16)

    blk_id = lax.broadcasted_iota(jnp.int32, (n_blocks, BLOCK), 0)
    key_row = lax.broadcasted_iota(jnp.int32, (BLOCK, BLOCK), 0)
    qry_col = lax.broadcasted_iota(jnp.int32, (BLOCK, BLOCK), 1)
    causal = key_row <= qry_col

    def pick_blocks(q, kbar):
        gates = _dot_nt(kbar, q, precision=HIGHEST)
        rank = jnp.zeros((n_blocks, BLOCK), F32)
        for m in range(n_blocks):
            gm = gates[m:m + 1, :]
            past = jnp.where(m < c, 1.0, 0.0)
            ge = jnp.where(gm >= gates, past, 0.0)
            gt = jnp.where(gm > gates, past, 0.0)
            rank = rank + jnp.where(blk_id > m, ge, gt)
        return jnp.where(rank < TOP_K, jnp.where(blk_id < c, 1.0, 0.0), 0.0)

    def attend_head(hd, sel, qb, nb, first_own):
        outs = []
        for qs in range(BLOCK // LANES):
            q_rows = slice(qs * LANES, (qs + 1) * LANES)
            qb_s = qb[q_rows, :]
            m = l = acc = None
            for n in range(nb):
                s = _dot_nt(kb_ref[n * BLOCK:(n + 1) * BLOCK, lanes_of(hd)], qb_s)
                keep = sel[n:n + 1, q_rows]
                if n >= first_own:
                    keep = keep + jnp.where(causal[:, q_rows], jnp.where(c == n, 1.0, 0.0), 0.0)
                s = jnp.where(keep > 0.5, s, NEG)
                bm = s.max(axis=0, keepdims=True)
                vt_n = vt_ref[lanes_of(hd), n * BLOCK:(n + 1) * BLOCK]
                if m is None:
                    m, p = bm, jnp.exp(s - bm)
                    l, acc = p.sum(axis=0, keepdims=True), _dot(vt_n, p.astype(BF16))
                else:
                    m_new = jnp.maximum(m, bm)
                    a, p = jnp.exp(m - m_new), jnp.exp(s - m_new)
                    l = a * l + p.sum(axis=0, keepdims=True)
                    acc = a * acc + _dot(vt_n, p.astype(BF16))
                    m = m_new
            outs.append((acc * (1.0 / l)).T)
        return jnp.concatenate(outs, axis=0).astype(o_ref.dtype)

    heads = range(MOBA_HEADS_PER_STEP)
    qs = [q_ref[:, lanes_of(hd)] for hd in heads]
    sels = [pick_blocks(qs[hd], kbar_ref[:, lanes_of(hd)]) for hd in heads]
    qbs = [(q * scale).astype(BF16) for q in qs]

    def attend(nb, first_own):
        o_ref[...] = jnp.concatenate([attend_head(hd, sels[hd], qbs[hd], nb, first_own) for hd in heads], axis=1)

    step = 2
    for nb in range(step, n_blocks + 1, step):
        pl.when(jnp.logical_and(c < nb, c >= nb - step))(functools.partial(attend, nb, nb - step))


def _moba_prompt(h, base, batch, seq):
    n_blocks = seq // BLOCK
    width = MOBA_HEADS_PER_STEP * HEAD_DIM
    groups = N_HEADS // MOBA_HEADS_PER_STEP
    return pl.pallas_call(
        functools.partial(_moba_prompt_kernel, n_blocks=n_blocks, scale=HEAD_DIM ** -0.5),
        grid=(batch, groups, n_blocks),
        in_specs=[
            pl.BlockSpec((BLOCK, width), lambda b, hg, c: (b * n_blocks + c, hg)),
            pl.BlockSpec((seq, width), lambda b, hg, c: (b, groups + hg)),
            pl.BlockSpec((seq, width), lambda b, hg, c: (b, 2 * groups + hg)),
            pl.BlockSpec(memory_space=pl.ANY),
        ],
        out_specs=pl.BlockSpec((BLOCK, width), lambda b, hg, c: (b * n_blocks + c, hg)),
        out_shape=jax.ShapeDtypeStruct(base.shape, base.dtype),
        input_output_aliases={3: 0},
        scratch_shapes=[pltpu.VMEM((seq, width), BF16), pltpu.VMEM((width, seq), BF16),
                        pltpu.VMEM((n_blocks, width), F32)],
        compiler_params=_params("parallel", "parallel", "arbitrary"),
        name="moba_prompt",
    )(h, h, h, base)


def _sample_topk_kernel(q_ref, kbar_ref, sel_ref):
    n_past = kbar_ref.shape[0]
    rows = q_ref.shape[0]
    lane_g = lax.broadcasted_iota(jnp.int32, (rows, n_past), 1).astype(F32)
    lane_o = lax.broadcasted_iota(jnp.int32, (rows, HEAD_DIM), 1)
    for hh in range(N_HEADS):
        qh = q_ref[:, hh * HEAD_DIM:(hh + 1) * HEAD_DIM]
        g = _dot_nt(qh, kbar_ref[:, hh, :], precision=HIGHEST)
        out = jnp.zeros((rows, HEAD_DIM), jnp.int32)
        for kk in range(TOP_K):
            best = jnp.max(g, axis=-1, keepdims=True)
            idx = jnp.min(jnp.where(g == best, lane_g, float(n_past)), axis=-1, keepdims=True)
            out = jnp.where(lane_o == kk, idx.astype(jnp.int32), out)
            g = jnp.where(lane_g == idx, -jnp.inf, g)
        sel_ref[hh] = out


def _sample_topk(q8, kbar):
    dec_batch, rows, width = q8.shape
    n_past = kbar.shape[1]
    return pl.pallas_call(
        _sample_topk_kernel,
        grid=(dec_batch,),
        in_specs=[
            pl.BlockSpec((None, rows, width), lambda b: (b, 0, 0)),
            pl.BlockSpec((None, n_past, N_HEADS, HEAD_DIM), lambda b: (b, 0, 0, 0)),
        ],
        out_specs=pl.BlockSpec((None, N_HEADS, rows, HEAD_DIM), lambda b: (b, 0, 0, 0)),
        out_shape=jax.ShapeDtypeStruct((dec_batch, N_HEADS, rows, HEAD_DIM), jnp.int32),
        compiler_params=_params("parallel"),
        name="sample_topk",
    )(q8, kbar)


def _sample_attn_kernel(pt_ref, sel_ref, q_ref, kn_ref, vn_ref, ck_hbm, cv_hbm, o_ref, kbuf, vbuf, sem,
                        *, layer, dec_seq, n_pages, scale):
    b = pl.program_id(0)
    hd = pl.program_id(1)
    step = b * N_HEADS + hd
    n_steps = pl.num_programs(0) * N_HEADS
    slot = step % 2
    sel_per_head = dec_seq * TOP_K
    n_blk = dec_seq * TOP_K

    def page_copies(bb, hh, sl):
        copies = []
        for i in range(n_blk):
            blk = sel_ref[(bb * N_HEADS + hh) * sel_per_head + i]
            for p in range(PAGES_PER_BLOCK):
                page = pt_ref[bb * n_pages + blk * PAGES_PER_BLOCK + p]
                dst = i * PAGES_PER_BLOCK + p
                copies.append(pltpu.make_async_copy(ck_hbm.at[layer, page, :, hh, :], kbuf.at[sl, dst], sem.at[0, sl]))
                copies.append(pltpu.make_async_copy(cv_hbm.at[layer, page, :, hh, :], vbuf.at[sl, dst], sem.at[1, sl]))
        return copies

    @pl.when(step == 0)
    def _():
        for cp in page_copies(b, hd, slot):
            cp.start()

    @pl.when(step + 1 < n_steps)
    def _():
        nxt = step + 1
        for cp in page_copies(nxt // N_HEADS, nxt % N_HEADS, 1 - slot):
            cp.start()

    for cp in page_copies(b, hd, slot):
        cp.wait()

    rows = q_ref.shape[0]
    qb = q_ref[...].astype(BF16)
    s_own_all = _dot_nt(qb, kn_ref[...].astype(BF16)) * scale
    vn = vn_ref[...].astype(BF16)
    col = lax.broadcasted_iota(jnp.int32, (1, rows), 1)

    def block(buf, i):
        return jnp.concatenate([buf[slot, i * PAGES_PER_BLOCK + p] for p in range(PAGES_PER_BLOCK)],
                               axis=0).astype(BF16)

    s_sel = [[_dot_nt(qb, block(kbuf, qi * TOP_K + kk))[qi:qi + 1, :] * scale for kk in range(TOP_K)]
             for qi in range(dec_seq)]
    p_own, p_sel, denoms = [], [], []
    for qi in range(dec_seq):
        s_own = jnp.where(col <= qi, s_own_all[qi:qi + 1, :], NEG)
        m = jnp.max(s_own, axis=-1, keepdims=True)
        for s in s_sel[qi]:
            m = jnp.maximum(m, jnp.max(s, axis=-1, keepdims=True))
        po = jnp.exp(s_own - m)
        ps = [jnp.exp(s - m) for s in s_sel[qi]]
        denom = jnp.sum(po, axis=-1, keepdims=True)
        for p in ps:
            denom = denom + jnp.sum(p, axis=-1, keepdims=True)
        p_own.append(po)
        p_sel.append(ps)
        denoms.append(denom)
    row = lax.broadcasted_iota(jnp.int32, (rows, HEAD_DIM), 0)
    result = jnp.zeros((rows, HEAD_DIM), F32)
    for qi in range(dec_seq):
        out = _dot(jnp.broadcast_to(p_own[qi], (rows, rows)).astype(BF16), vn)
        for kk in range(TOP_K):
            p = jnp.broadcast_to(p_sel[qi][kk], (rows, BLOCK)).astype(BF16)
            out = out + _dot(p, block(vbuf, qi * TOP_K + kk))
        result = jnp.where(row == qi, out / denoms[qi], result)
    o_ref[...] = result


def _sample_attn(q8, k8, v8, cache_k, cache_v, pt_flat, sel_flat, layer, dec_seq, n_pages):
    dec_batch, rows, _ = q8.shape
    n_sel_pages = dec_seq * TOP_K * PAGES_PER_BLOCK
    head_spec = pl.BlockSpec((None, rows, HEAD_DIM), lambda b, hh, pt, sel: (b, 0, hh))
    hbm = pl.BlockSpec(memory_space=pl.ANY)
    return pl.pallas_call(
        functools.partial(_sample_attn_kernel, layer=layer, dec_seq=dec_seq, n_pages=n_pages,
                          scale=HEAD_DIM ** -0.5),
        grid_spec=pltpu.PrefetchScalarGridSpec(
            num_scalar_prefetch=2,
            grid=(dec_batch, N_HEADS),
            in_specs=[head_spec, head_spec, head_spec, hbm, hbm],
            out_specs=head_spec,
            scratch_shapes=[pltpu.VMEM((2, n_sel_pages, PAGE_SIZE, HEAD_DIM), F32),
                            pltpu.VMEM((2, n_sel_pages, PAGE_SIZE, HEAD_DIM), F32),
                            pltpu.SemaphoreType.DMA((2, 2))],
        ),
        out_shape=jax.ShapeDtypeStruct(q8.shape, F32),
        compiler_params=_params("arbitrary", "arbitrary"),
        name="sample_attn",
    )(pt_flat, sel_flat, q8, k8, v8, cache_k, cache_v)


def _s5_tables(a_re, a_im, log_dt, b_re, b_im, c_re, c_im):
    L = SSM_CHUNK
    lam = lax.complex(a_re, a_im)
    dt = jnp.exp(log_dt)[..., None]
    lam_bar = jnp.exp(lam * dt)
    b_bar = ((lam_bar - 1.0) / lam)[..., None] * lax.complex(b_re, b_im)
    c_mat = lax.complex(c_re, c_im)
    pw = [jnp.ones_like(lam_bar)]
    for _ in range(L):
        pw.append(pw[-1] * lam_bar)
    pows = jnp.stack(pw, axis=2)
    z = (c_mat.transpose(0, 1, 3, 2)[:, :, :, None, :]
         * pows.transpose(0, 1, 3, 2)[:, :, :, :, None])
    dd, gg = z.shape[0], z.shape[1]
    z = z.reshape(dd, gg, SSM_STATE, (L + 1) * SSM_GROUP)
    zt = jnp.concatenate([jnp.real(z), -jnp.imag(z)], axis=2)
    bst = jnp.concatenate([jnp.real(b_bar), jnp.imag(b_bar)], axis=2)
    gw = L * SSM_GROUP
    r0 = jnp.einsum('dgpi,dgpn->dgin', bst, zt[..., :gw], precision=HIGHEST)
    tt = jnp.stack([jnp.pad(r0[..., :gw - t * SSM_GROUP], ((0, 0), (0, 0), (0, 0), (t * SSM_GROUP, 0)))
                    for t in range(L)], axis=2).reshape(dd, gg, gw, gw)
    nt = zt[..., SSM_GROUP:]
    mcx = (pows[:, :, L - 1::-1][:, :, :, None, :]
           * b_bar.transpose(0, 1, 3, 2)[:, :, None])
    mcx = mcx.reshape(dd, gg, gw, SSM_STATE)
    mt = jnp.concatenate([jnp.real(mcx), jnp.imag(mcx)], axis=-1)

    def lanes(zc):
        return (jnp.concatenate([jnp.real(zc), jnp.real(zc)], axis=-1),
                jnp.concatenate([-jnp.imag(zc), jnp.imag(zc)], axis=-1))

    steps = []
    cur = pows[:, :, L]
    for _ in range(8):
        steps.append(cur)
        cur = cur * cur
    ar, ai = lanes(jnp.stack(steps, axis=2))
    return dict(tt=tt, mt=mt, nt=nt, ar=ar, ai=ai, pows=pows)


def _chunk_permutation():
    n = SSM_CHUNK * LANES
    src = jnp.arange(n)
    t, g, ch = src // LANES, (src % LANES) // SSM_GROUP, src % SSM_GROUP
    dst = g * (SSM_CHUNK * SSM_GROUP) + t * SSM_GROUP + ch
    return (dst[:, None] == jnp.arange(n)[None, :]).astype(BF16)


def _s5_prompt_kernel(x_ref, perm_ref, tt_ref, mt_ref, nt_ref, ar_ref, ai_ref, d_ref, base_ref, y_ref, hl_ref,
                      *, n_seq, n_chunks):
    del base_ref
    L = SSM_CHUNK
    gw = L * SSM_GROUP
    rows = n_seq * n_chunks
    u_tok = jnp.concatenate([x_ref[:, t, :].astype(BF16) for t in range(L)], axis=1)
    u_grp = _dot(u_tok, perm_ref[...]).astype(BF16)
    kpos = lax.broadcasted_iota(jnp.int32, (rows, 2 * SSM_STATE), 0) % n_chunks
    y_parts = []
    for g in range(GROUPS_PER_LANE_BLOCK):
        ub = u_grp[:, g * gw:(g + 1) * gw]
        w = _dot(ub, mt_ref[g])
        stride, i = 1, 0
        while stride < n_chunks:
            sh = jnp.where(kpos >= stride, pltpu.roll(w, stride, 0), 0.0)
            w = w + sh * ar_ref[g, i:i + 1, :] + pltpu.roll(sh, SSM_STATE, 1) * ai_ref[g, i:i + 1, :]
            stride, i = stride * 2, i + 1
        h_in = jnp.where(kpos >= 1, pltpu.roll(w, 1, 0), 0.0)
        y_parts.append((_dot(ub, tt_ref[g]) + _dot(h_in.astype(BF16), nt_ref[g])).astype(BF16))
        for n in range(n_seq):
            hl_ref[g, n:n + 1, :] = w[(n + 1) * n_chunks - 1:(n + 1) * n_chunks, :]
    y_tok = _dot_nt(jnp.concatenate(y_parts, axis=1), perm_ref[...])
    for t in range(L):
        y_ref[:, t, :] = y_tok[:, t * LANES:(t + 1) * LANES] + d_ref[...] * x_ref[:, t, :]


def _s5_prompt(h3, base3, perm, tt, mt, nt, ar, ai, d_lane, layer, n_seq_total, n_chunks, u_lane_block0,
               seq_per_step=2):
    groups = tt.shape[1]
    lane_blocks = groups // GROUPS_PER_LANE_BLOCK
    rows = seq_per_step * n_chunks
    p2 = 2 * SSM_STATE
    gw = SSM_CHUNK * SSM_GROUP
    gpb = GROUPS_PER_LANE_BLOCK
    n_perm = SSM_CHUNK * LANES
    return pl.pallas_call(
        functools.partial(_s5_prompt_kernel, n_seq=seq_per_step, n_chunks=n_chunks),
        grid=(n_seq_total // seq_per_step, lane_blocks),
        in_specs=[
            pl.BlockSpec((rows, SSM_CHUNK, LANES), lambda s, g: (s, 0, u_lane_block0 + g)),
            pl.BlockSpec((n_perm, n_perm), lambda s, g: (0, 0)),
            pl.BlockSpec((None, gpb, gw, gw), lambda s, g: (layer, g, 0, 0)),
            pl.BlockSpec((None, gpb, gw, p2), lambda s, g: (layer, g, 0, 0)),
            pl.BlockSpec((None, gpb, p2, gw), lambda s, g: (layer, g, 0, 0)),
            pl.BlockSpec((None, gpb, 8, p2), lambda s, g: (layer, g, 0, 0)),
            pl.BlockSpec((None, gpb, 8, p2), lambda s, g: (layer, g, 0, 0)),
            pl.BlockSpec((None, None, 1, LANES), lambda s, g: (layer, g, 0, 0)),
            pl.BlockSpec(memory_space=pl.ANY),
        ],
        out_specs=[
            pl.BlockSpec((rows, SSM_CHUNK, LANES), lambda s, g: (s, 0, g)),
            pl.BlockSpec((None, gpb, seq_per_step, p2), lambda s, g: (s, g, 0, 0)),
        ],
        out_shape=[
            jax.ShapeDtypeStruct(base3.shape, base3.dtype),
            jax.ShapeDtypeStruct((n_seq_total // seq_per_step, groups, seq_per_step, p2), F32),
        ],
        input_output_aliases={8: 0},
        compiler_params=_params("parallel", "arbitrary"),
        name="s5_prompt",
    )(h3, perm, tt, mt, nt, ar, ai, d_lane, base3)


def _s5_sample_kernel(u_ref, h0_ref, tt_ref, mt_ref, nt_ref, ar_ref, ai_ref, d_ref, y_ref, hn_ref):
    groups = u_ref.shape[0]

    def body(g, carry):
        u = u_ref[g]
        h0 = h0_ref[g]
        y_ref[g] = _dot(u, tt_ref[g], HIGHEST) + _dot(h0, nt_ref[g], HIGHEST) + d_ref[g] * u
        hn_ref[g] = (h0 * ar_ref[g] + pltpu.roll(h0, SSM_STATE, 1) * ai_ref[g]
                     + _dot(u, mt_ref[g], HIGHEST))
        return carry

    lax.fori_loop(0, groups, body, 0)


def _s5_sample(u_g, h0_g, tt_s, mt_s, nt_s, ar_s, ai_s, d_s):
    args = (u_g, h0_g, tt_s, mt_s, nt_s, ar_s, ai_s, d_s)
    full = lambda a: pl.BlockSpec(a.shape, lambda i, nd=a.ndim: (0,) * nd)
    return pl.pallas_call(
        _s5_sample_kernel,
        grid=(1,),
        in_specs=[full(a) for a in args],
        out_specs=[full(u_g), full(h0_g)],
        out_shape=[jax.ShapeDtypeStruct(u_g.shape, F32), jax.ShapeDtypeStruct(h0_g.shape, F32)],
        compiler_params=_params("arbitrary"),
        name="s5_sample",
    )(*args)


def _mix_ln_kernel(a_ref, ys_ref, ga_ref, gb_ref, x_ref, wa_ref, wv_ref, wg_ref, wo_ref, g_ref, b_ref,
                   o_ref, zb_ref, acc_ref, *, alpha):
    j = pl.program_id(1)

    @pl.when(j == 0)
    def _():
        zb_ref[...] = jax.nn.gelu(ys_ref[...]).astype(BF16)
        acc_ref[...] = jnp.zeros_like(acc_ref)

    zb = zb_ref[...]
    y_a = _dot(a_ref[...], wa_ref[...])
    y_b = _dot(zb, wv_ref[...]) * jax.nn.sigmoid(_dot(zb, wg_ref[...]))
    mixed = jax.nn.sigmoid(ga_ref[...]) * y_a + jax.nn.sigmoid(gb_ref[...]) * y_b
    acc_ref[...] += _dot(mixed.astype(BF16), wo_ref[...])

    @pl.when(j == pl.num_programs(1) - 1)
    def _():
        o_ref[...] = _layer_norm(alpha * x_ref[...] + acc_ref[...], g_ref[...], b_ref[...])


def _mix_ln(attn, y_ssm, h, x, wa, wv, wg, wo, gain, bias, layer, alpha, tn=512):
    t, d = x.shape
    aw = attn.shape[1]
    sw = y_ssm.shape[1]
    tm = ROW_TILE
    ga_blk = (3 * ATTN_WIDTH + sw) // tn
    gb_blk = ga_blk + d // tn
    return pl.pallas_call(
        functools.partial(_mix_ln_kernel, alpha=alpha),
        grid=(t // tm, d // tn),
        in_specs=[
            pl.BlockSpec((tm, aw), lambda i, j: (i, 0)),
            pl.BlockSpec((tm, sw), lambda i, j: (i, 0)),
            pl.BlockSpec((tm, tn), lambda i, j: (i, ga_blk + j)),
            pl.BlockSpec((tm, tn), lambda i, j: (i, gb_blk + j)),
            pl.BlockSpec((tm, d), lambda i, j: (i, 0)),
            pl.BlockSpec((None, aw, tn), lambda i, j: (layer, 0, j)),
            pl.BlockSpec((None, sw, tn), lambda i, j: (layer, 0, j)),
            pl.BlockSpec((None, sw, tn), lambda i, j: (layer, 0, j)),
            pl.BlockSpec((None, tn, d), lambda i, j: (layer, j, 0)),
            pl.BlockSpec((1, d), lambda i, j: (0, 0)),
            pl.BlockSpec((1, d), lambda i, j: (0, 0)),
        ],
        out_specs=pl.BlockSpec((tm, d), lambda i, j: (i, 0)),
        out_shape=jax.ShapeDtypeStruct((t, d), F32),
        scratch_shapes=[pltpu.VMEM((tm, sw), BF16), pltpu.VMEM((tm, d), F32)],
        compiler_params=_params("parallel", "arbitrary"),
        name="mix_ln",
    )(attn, y_ssm, h, h, x, wa, wv, wg, wo, gain, bias)


def kernel(x_prompt, x_sample, cache_k, cache_v, state_ssm_re, state_ssm_im, page_table, w_in, w_attn_out, w_glu_v, w_glu_g, w_out, ssm_a_re, ssm_a_im, ssm_log_dt, ssm_b_re, ssm_b_im, ssm_c_re, ssm_c_im, ssm_d, ffn1_gate, ffn1_up, ffn1_down, ffn2_gate, ffn2_up, ffn2_down, ln_gain, ln_bias):
    batch, seq, d_model = x_prompt.shape
    dec_batch, dec_seq, _ = x_sample.shape
    depth = w_in.shape[0]
    n_pages = page_table.shape[1]
    past_len = n_pages * PAGE_SIZE
    ssm_width = ssm_d.shape[1]
    groups = ssm_width // SSM_GROUP
    L, CH = SSM_CHUNK, SSM_GROUP
    n_chunks = seq // L
    t_p = batch * seq
    t_s = dec_batch * dec_seq
    t_pad = -(-(t_p + t_s) // DENSE_ROW_TILE) * DENSE_ROW_TILE
    t_tail = t_pad - t_p
    assert past_len % BLOCK == 0 and past_len // BLOCK >= TOP_K
    assert seq % (2 * BLOCK) == 0 and t_pad % L == 0 and batch % 2 == 0
    assert dec_seq <= 8 and dec_seq <= L and n_pages % KBAR_PAGES_PER_STEP == 0
    assert (3 * ATTN_WIDTH) % LANES == 0 and groups % GROUPS_PER_LANE_BLOCK == 0
    alpha = (2.0 * depth) ** 0.25

    x = jnp.concatenate([x_prompt.reshape(t_p, d_model), x_sample.reshape(t_s, d_model),
                         jnp.zeros((t_tail - t_s, d_model), F32)], axis=0)

    pos = jnp.concatenate([jnp.tile(jnp.arange(seq, dtype=jnp.int32), batch),
                           jnp.tile(past_len + jnp.arange(dec_seq, dtype=jnp.int32), dec_batch),
                           jnp.zeros((t_tail - t_s,), jnp.int32)])
    rope_c, rope_s1, rope_s2 = _rope_tables(pos)

    w_in_b, w_ao_b, w_gv_b, w_gg_b, w_out_b = (w.astype(BF16) for w in (w_in, w_attn_out, w_glu_v, w_glu_g, w_out))
    f1g, f1u, f1d, f2g, f2u, f2d = (w.astype(BF16) for w in
                                    (ffn1_gate, ffn1_up, ffn1_down, ffn2_gate, ffn2_up, ffn2_down))

    tabs = _s5_tables(ssm_a_re, ssm_a_im, ssm_log_dt, ssm_b_re, ssm_b_im, ssm_c_re, ssm_c_im)
    tt_b, mt_b, nt_b = tabs['tt'].astype(BF16), tabs['mt'].astype(BF16), tabs['nt'].astype(BF16)
    perm = _chunk_permutation()
    d_lane = ssm_d.reshape(depth, ssm_width // LANES, 1, LANES)
    d_g = ssm_d.reshape(depth, groups, 1, CH)
    ws = dec_seq * CH
    tt_s = tabs['tt'][:, :, :ws, :ws]
    nt_s = tabs['nt'][:, :, :, :ws]
    mt_s = tabs['mt'][:, :, (L - dec_seq) * CH:, :]
    lam_ds = tabs['pows'][:, :, dec_seq]
    ar_s = jnp.concatenate([jnp.real(lam_ds), jnp.real(lam_ds)], axis=-1)[:, :, None, :]
    ai_s = jnp.concatenate([-jnp.imag(lam_ds), jnp.imag(lam_ds)], axis=-1)[:, :, None, :]
    d_s = jnp.tile(d_g, (1, 1, 1, dec_seq))
    h0_s = jnp.concatenate([state_ssm_re, state_ssm_im], axis=-1).transpose(0, 2, 1, 3)

    pt_flat = page_table.reshape(-1)
    row_pad = 8 - dec_seq

    def pad_rows(a):
        return jnp.pad(a, ((0, 0), (0, row_pad), (0, 0)))

    def tail_base(tail_rows):
        width, dtype = tail_rows.shape[1], tail_rows.dtype
        return jnp.concatenate([jnp.zeros((t_p, width), dtype), tail_rows,
                                jnp.zeros((t_tail - t_s, width), dtype)], axis=0)

    u0 = 3 * ATTN_WIDTH
    kp_l, vp_l, hrp_l, hip_l, ks_l, vs_l, hrs_l, his_l = ([] for _ in range(8))
    for l in range(depth):
        gains = [ln_gain[l, i][None, :] for i in range(3)]
        biases = [ln_bias[l, i][None, :] for i in range(3)]
        x, kbar = _ffn_ln(x, f1g, f1u, f1d, gains[0], biases[0], l, alpha,
                          kbar_from=(cache_k, pt_flat, dec_batch, n_pages))
        h, k3, v3 = _in_proj(x, w_in_b, rope_c, rope_s1, rope_s2, l)

        hs = h[t_p:t_p + t_s]
        q_s = hs[:, :ATTN_WIDTH].reshape(dec_batch, dec_seq, ATTN_WIDTH)
        k_s = hs[:, ATTN_WIDTH:2 * ATTN_WIDTH].reshape(dec_batch, dec_seq, ATTN_WIDTH)
        v_s = hs[:, 2 * ATTN_WIDTH:3 * ATTN_WIDTH].reshape(dec_batch, dec_seq, ATTN_WIDTH)
        q8, k8, v8 = pad_rows(q_s), pad_rows(k_s), pad_rows(v_s)
        sel = _sample_topk(q8, kbar)[:, :, :dec_seq, :TOP_K]
        attn_s = _sample_attn(q8, k8, v8, cache_k, cache_v, pt_flat, sel.reshape(-1), l, dec_seq, n_pages)
        attn_s = attn_s[:, :dec_seq].reshape(t_s, ATTN_WIDTH)
        attn = _moba_prompt(h, tail_base(attn_s.astype(BF16)), batch, seq)

        us_g = hs[:, u0:u0 + ssm_width].reshape(dec_batch, dec_seq, groups, CH)
        us_g = us_g.transpose(2, 0, 1, 3).reshape(groups, dec_batch, ws)
        ys_g, hn_g = _s5_sample(us_g, h0_s[l], tt_s[l], mt_s[l], nt_s[l], ar_s[l], ai_s[l], d_s[l])
        y_s = ys_g.reshape(groups, dec_batch, dec_seq, CH).transpose(1, 2, 0, 3).reshape(t_s, ssm_width)
        h3 = h.reshape(t_pad // L, L, h.shape[1])
        y3, hl4 = _s5_prompt(h3, tail_base(y_s).reshape(t_pad // L, L, ssm_width), perm, tt_b, mt_b, nt_b,
                             tabs['ar'], tabs['ai'], d_lane, l, batch, n_chunks, u0 // LANES)
        y_ssm = y3.reshape(t_pad, ssm_width)

        x = _mix_ln(attn, y_ssm, h, x, w_ao_b, w_gv_b, w_gg_b, w_out_b, gains[1], biases[1], l, alpha)
        x = _ffn_ln(x, f2g, f2u, f2d, gains[2], biases[2], l, alpha, tf=512)

        kp_l.append(k3[:t_p].reshape(batch, seq, N_HEADS, HEAD_DIM))
        vp_l.append(v3[:t_p].reshape(batch, seq, N_HEADS, HEAD_DIM))
        hl = hl4.transpose(0, 2, 1, 3).reshape(batch, groups, 2 * SSM_STATE)
        hrp_l.append(hl[..., :SSM_STATE])
        hip_l.append(hl[..., SSM_STATE:])
        ks_l.append(k3[t_p:t_p + t_s].reshape(dec_batch, dec_seq, N_HEADS, HEAD_DIM))
        vs_l.append(v3[t_p:t_p + t_s].reshape(dec_batch, dec_seq, N_HEADS, HEAD_DIM))
        hn = hn_g.transpose(1, 0, 2)
        hrs_l.append(hn[..., :SSM_STATE])
        his_l.append(hn[..., SSM_STATE:])

    y_prompt = x[:t_p].reshape(batch, seq, d_model)
    y_sample = x[t_p:t_p + t_s].reshape(dec_batch, dec_seq, d_model)
    return (y_prompt, y_sample,
            jnp.stack(kp_l), jnp.stack(vp_l), jnp.stack(hrp_l), jnp.stack(hip_l),
            jnp.stack(ks_l), jnp.stack(vs_l), jnp.stack(hrs_l), jnp.stack(his_l))
```

```python
import functools

import jax
import jax.numpy as jnp
from jax import lax
from jax.experimental import pallas as pl
from jax.experimental.pallas import tpu as pltpu

N_HEADS = 8
HEAD_DIM = 128
ATTN_WIDTH = N_HEADS * HEAD_DIM
ROT_DIM = HEAD_DIM // 4
ROT_HALF = ROT_DIM // 2
ROPE_THETA = 500000.0
BLOCK = 256
TOP_K = 3
PAGE_SIZE = 128
PAGES_PER_BLOCK = BLOCK // PAGE_SIZE
SSM_GROUP = 16
SSM_STATE = 64
SSM_CHUNK = 16
LANES = 128
GROUPS_PER_LANE_BLOCK = LANES // SSM_GROUP
LN_EPS = 1e-5
NEG = -1e30

ROW_TILE = 528
DENSE_ROW_TILE = 2 * ROW_TILE
VMEM_LIMIT = 56 * 1024 * 1024

F32 = jnp.float32
BF16 = jnp.bfloat16
HIGHEST = lax.Precision.HIGHEST


def _params(*sem):
    return pltpu.CompilerParams(dimension_semantics=sem, vmem_limit_bytes=VMEM_LIMIT)


def _layer_norm(r, g, b):
    mu = jnp.mean(r, axis=-1, keepdims=True)
    d = r - mu
    var = jnp.mean(d * d, axis=-1, keepdims=True)
    return d * lax.rsqrt(var + LN_EPS) * g + b


def _dot(a, b, precision=None):
    return jnp.dot(a, b, precision=precision, preferred_element_type=F32)


def _dot_nt(a, b, precision=None):
    return lax.dot_general(a, b, (((1,), (1,)), ((), ())), precision=precision,
                           preferred_element_type=F32)


KBAR_PAGES_PER_STEP = 8


def _ffn_ln_kernel(*refs, alpha, kbar_steps):
    n_pages = KBAR_PAGES_PER_STEP if kbar_steps else 0
    refs = refs[1:] if kbar_steps else refs
    x_ref, wg_ref, wu_ref, wd_ref, g_ref, b_ref = refs[:6]
    pages = refs[6:6 + n_pages]
    o_ref = refs[6 + n_pages]
    kbar_ref = refs[7 + n_pages] if kbar_steps else None
    xb_ref = refs[-1]
    i = pl.program_id(0)
    j = pl.program_id(1)

    @pl.when(j == 0)
    def _():
        xb_ref[...] = x_ref[...].astype(BF16)
        o_ref[...] = jnp.zeros_like(o_ref)

    xb = xb_ref[...]
    gate = _dot(xb, wg_ref[...].astype(BF16))
    up = _dot(xb, wu_ref[...].astype(BF16))
    hid = (gate * jax.nn.sigmoid(gate) * up).astype(BF16)
    o_ref[...] += _dot(hid, wd_ref[...].astype(BF16))

    if kbar_steps:
        @pl.when(i * pl.num_programs(1) + j < kbar_steps)
        def _():
            for blk in range(n_pages // PAGES_PER_BLOCK):
                tot = None
                for p in range(PAGES_PER_BLOCK):
                    part = jnp.sum(pages[blk * PAGES_PER_BLOCK + p][...], axis=0)
                    tot = part if tot is None else tot + part
                kbar_ref[blk] = tot * (1.0 / BLOCK)

    @pl.when(j == pl.num_programs(1) - 1)
    def _():
        o_ref[...] = _layer_norm(alpha * x_ref[...] + 0.5 * o_ref[...], g_ref[...], b_ref[...])


def _ffn_ln(x, wg, wu, wd, gain, bias, layer, alpha, tf=256, kbar_from=None):
    t, d = x.shape
    f = wg.shape[-1]
    tm = DENSE_ROW_TILE
    grid = (t // tm, f // tf)
    in_specs = [
        pl.BlockSpec((tm, d), lambda i, j, *_: (i, 0), pipeline_mode=pl.Buffered(1)),
        pl.BlockSpec((None, d, tf), lambda i, j, *_: (layer, 0, j)),
        pl.BlockSpec((None, d, tf), lambda i, j, *_: (layer, 0, j)),
        pl.BlockSpec((None, tf, d), lambda i, j, *_: (layer, j, 0)),
        pl.BlockSpec((1, d), lambda i, j, *_: (0, 0)),
        pl.BlockSpec((1, d), lambda i, j, *_: (0, 0)),
    ]
    out_specs = pl.BlockSpec((tm, d), lambda i, j, *_: (i, 0))
    out_shape = jax.ShapeDtypeStruct((t, d), F32)
    scratch = [pltpu.VMEM((tm, d), BF16)]
    if kbar_from is None:
        return pl.pallas_call(
            functools.partial(_ffn_ln_kernel, alpha=alpha, kbar_steps=0),
            grid=grid, in_specs=in_specs, out_specs=out_specs, out_shape=out_shape, scratch_shapes=scratch,
            compiler_params=_params("parallel", "arbitrary"), name="ffn_ln",
        )(x, wg, wu, wd, gain, bias)

    cache_k, pt_flat, dec_batch, n_pages = kbar_from
    pps = KBAR_PAGES_PER_STEP
    kbar_steps = dec_batch * n_pages // pps
    steps_per_seq = n_pages // pps
    blocks_per_step = pps // PAGES_PER_BLOCK
    assert kbar_steps <= grid[0] * grid[1]

    def side_step(i, j):
        return jnp.minimum(i * grid[1] + j, kbar_steps - 1)

    def page_spec(p):
        return pl.BlockSpec((None, None, PAGE_SIZE, N_HEADS, HEAD_DIM),
                            lambda i, j, pt: (layer, pt[side_step(i, j) * pps + p], 0, 0, 0))

    kbar_spec = pl.BlockSpec(
        (None, blocks_per_step, N_HEADS, HEAD_DIM),
        lambda i, j, pt: (side_step(i, j) // steps_per_seq, side_step(i, j) % steps_per_seq, 0, 0))
    out_specs = pl.BlockSpec((tm, d), lambda i, j, *_: (i, 0), pipeline_mode=pl.Buffered(1))
    return pl.pallas_call(
        functools.partial(_ffn_ln_kernel, alpha=alpha, kbar_steps=kbar_steps),
        grid_spec=pltpu.PrefetchScalarGridSpec(
            num_scalar_prefetch=1, grid=grid,
            in_specs=in_specs + [page_spec(p) for p in range(pps)],
            out_specs=[out_specs, kbar_spec], scratch_shapes=scratch),
        out_shape=[out_shape, jax.ShapeDtypeStruct((dec_batch, n_pages // PAGES_PER_BLOCK, N_HEADS, HEAD_DIM), F32)],
        compiler_params=_params("arbitrary", "arbitrary"), name="ffn_ln_kbar",
    )(pt_flat, x, wg, wu, wd, gain, bias, *([cache_k] * pps))


INPROJ_HEADS_PER_TILE = 4
INPROJ_TILES_PER_SECTION = N_HEADS // INPROJ_HEADS_PER_TILE


def _inproj_kernel(x_ref, w_ref, c_ref, s1_ref, s2_ref, o_ref, k3_ref, v3_ref, xb_ref):
    j = pl.program_id(1)
    tps = INPROJ_TILES_PER_SECTION

    @pl.when(j == 0)
    def _():
        xb_ref[...] = x_ref[...].astype(BF16)

    def heads():
        half = INPROJ_HEADS_PER_TILE // 2
        for part in range(2):
            cols = slice(part * half * HEAD_DIM, (part + 1) * half * HEAD_DIM)
            y = _dot(xb_ref[...], w_ref[:, cols].astype(BF16))
            for hl in range(half):
                yield part * half + hl, y[:, hl * HEAD_DIM:(hl + 1) * HEAD_DIM]

    def rope(yh):
        return (yh * c_ref[...]
                + pltpu.roll(yh, ROT_HALF, 1) * s1_ref[...]
                + pltpu.roll(yh, HEAD_DIM - ROT_HALF, 1) * s2_ref[...])

    @pl.when(j < tps)
    def _():
        for hh, yh in heads():
            o_ref[:, hh * HEAD_DIM:(hh + 1) * HEAD_DIM] = rope(yh)

    for tile in range(tps):
        head0 = tile * INPROJ_HEADS_PER_TILE

        @pl.when(j == tps + tile)
        def _():
            for hh, yh in heads():
                kh = rope(yh)
                o_ref[:, hh * HEAD_DIM:(hh + 1) * HEAD_DIM] = kh
                k3_ref[:, head0 + hh, :] = kh

        @pl.when(j == 2 * tps + tile)
        def _():
            for hh, yh in heads():
                o_ref[:, hh * HEAD_DIM:(hh + 1) * HEAD_DIM] = yh
                v3_ref[:, head0 + hh, :] = yh

    @pl.when(j >= 3 * tps)
    def _():
        o_ref[...] = _dot(xb_ref[...], w_ref[...].astype(BF16))


def _in_proj(x, w_in, rope_c, rope_s1, rope_s2, k_all, v_all, layer):
    t, d = x.shape
    n = w_in.shape[-1]
    tm, tn = DENSE_ROW_TILE, INPROJ_HEADS_PER_TILE * HEAD_DIM
    kv_spec = pl.BlockSpec((None, tm, N_HEADS, HEAD_DIM), lambda i, j: (layer, i, 0, 0))
    hbm = pl.BlockSpec(memory_space=pl.ANY)

    def body(x_ref, w_ref, c_ref, s1_ref, s2_ref, k_in, v_in, o_ref, k3_ref, v3_ref, xb_ref):
        del k_in, v_in
        _inproj_kernel(x_ref, w_ref, c_ref, s1_ref, s2_ref, o_ref, k3_ref, v3_ref, xb_ref)

    return pl.pallas_call(
        body,
        grid=(t // tm, n // tn),
        in_specs=[
            pl.BlockSpec((tm, d), lambda i, j: (i, 0), pipeline_mode=pl.Buffered(1)),
            pl.BlockSpec((None, d, tn), lambda i, j: (layer, 0, j)),
            pl.BlockSpec((tm, HEAD_DIM), lambda i, j: (i, 0)),
            pl.BlockSpec((tm, HEAD_DIM), lambda i, j: (i, 0)),
            pl.BlockSpec((tm, HEAD_DIM), lambda i, j: (i, 0)),
            hbm, hbm,
        ],
        out_specs=[pl.BlockSpec((tm, tn), lambda i, j: (i, j)), kv_spec, kv_spec],
        out_shape=[jax.ShapeDtypeStruct((t, n), F32),
                   jax.ShapeDtypeStruct(k_all.shape, F32), jax.ShapeDtypeStruct(v_all.shape, F32)],
        input_output_aliases={5: 1, 6: 2},
        scratch_shapes=[pltpu.VMEM((tm, d), BF16)],
        compiler_params=_params("parallel", "arbitrary"),
        name="in_proj_rope",
    )(x, w_in, rope_c, rope_s1, rope_s2, k_all, v_all)


def _rope_tables(pos):
    inv = jnp.power(ROPE_THETA, -jnp.arange(ROT_HALF, dtype=F32) * 2.0 / ROT_DIM)
    ang = pos.astype(F32)[:, None] * inv[None, :]
    cos, sin = jnp.cos(ang), jnp.sin(ang)
    t = pos.shape[0]
    ones = jnp.ones((t, HEAD_DIM - ROT_DIM), F32)
    zeros_h = jnp.zeros((t, ROT_HALF), F32)
    zeros_r = jnp.zeros((t, HEAD_DIM - ROT_DIM), F32)
    c = jnp.concatenate([cos, cos, ones], axis=1)
    s1 = jnp.concatenate([zeros_h, sin, zeros_r], axis=1)
    s2 = jnp.concatenate([-sin, zeros_h, zeros_r], axis=1)
    return c, s1, s2


MOBA_HEADS_PER_STEP = 4


def _moba_prompt_kernel(q_ref, k_ref, v_ref, base_ref, o_ref, kb_ref, vt_ref, kbar_ref, *, n_blocks, scale):
    del base_ref
    c = pl.program_id(2)
    lanes_of = lambda hd: slice(hd * HEAD_DIM, (hd + 1) * HEAD_DIM)

    @pl.when(c == 0)
    def _():
        for n in range(n_blocks):
            kbar_ref[n:n + 1, :] = jnp.mean(k_ref[n * BLOCK:(n + 1) * BLOCK, :], axis=0, keepdims=True)
        kb_ref[...] = k_ref[...].astype(BF16)
        vt_ref[...] = v_ref[...].T.astype(BF16)

    blk_id = lax.broadcasted_iota(jnp.int32, (n_blocks, BLOCK), 0)
    key_row = lax.broadcasted_iota(jnp.int32, (BLOCK, BLOCK), 0)
    qry_col = lax.broadcasted_iota(jnp.int32, (BLOCK, BLOCK), 1)
    causal = key_row <= qry_col

    def pick_blocks(q, kbar):
        gates = _dot_nt(kbar, q, precision=HIGHEST)
        rank = jnp.zeros((n_blocks, BLOCK), F32)
        for m in range(n_blocks):
            gm = gates[m:m + 1, :]
            past = jnp.where(m < c, 1.0, 0.0)
            ge = jnp.where(gm >= gates, past, 0.0)
            gt = jnp.where(gm > gates, past, 0.0)
            rank = rank + jnp.where(blk_id > m, ge, gt)
        return jnp.where(rank < TOP_K, jnp.where(blk_id < c, 1.0, 0.0), 0.0)

    def attend_head(hd, sel, qb, nb, first_own):
        outs = []
        for qs in range(BLOCK // LANES):
            q_rows = slice(qs * LANES, (qs + 1) * LANES)
            qb_s = qb[q_rows, :]
            m = l = acc = None
            for n in range(nb):
                s = _dot_nt(kb_ref[n * BLOCK:(n + 1) * BLOCK, lanes_of(hd)], qb_s)
                keep = sel[n:n + 1, q_rows]
                if n >= first_own:
                    keep = keep + jnp.where(causal[:, q_rows], jnp.where(c == n, 1.0, 0.0), 0.0)
                s = jnp.where(keep > 0.5, s, NEG)
                bm = s.max(axis=0, keepdims=True)
                vt_n = vt_ref[lanes_of(hd), n * BLOCK:(n + 1) * BLOCK]
                if m is None:
                    m, p = bm, jnp.exp(s - bm)
                    l, acc = p.sum(axis=0, keepdims=True), _dot(vt_n, p.astype(BF16))
                else:
                    m_new = jnp.maximum(m, bm)
                    a, p = jnp.exp(m - m_new), jnp.exp(s - m_new)
                    l = a * l + p.sum(axis=0, keepdims=True)
                    acc = a * acc + _dot(vt_n, p.astype(BF16))
                    m = m_new
            outs.append((acc * (1.0 / l)).T)
        return jnp.concatenate(outs, axis=0).astype(o_ref.dtype)

    heads = range(MOBA_HEADS_PER_STEP)
    qs = [q_ref[:, lanes_of(hd)] for hd in heads]
    sels = [pick_blocks(qs[hd], kbar_ref[:, lanes_of(hd)]) for hd in heads]
    qbs = [(q * scale).astype(BF16) for q in qs]

    def attend(nb, first_own):
        o_ref[...] = jnp.concatenate([attend_head(hd, sels[hd], qbs[hd], nb, first_own) for hd in heads], axis=1)

    step = 2
    for nb in range(step, n_blocks + 1, step):
        pl.when(jnp.logical_and(c < nb, c >= nb - step))(functools.partial(attend, nb, nb - step))


def _moba_prompt(h, base, batch, seq):
    n_blocks = seq // BLOCK
    width = MOBA_HEADS_PER_STEP * HEAD_DIM
    groups = N_HEADS // MOBA_HEADS_PER_STEP
    return pl.pallas_call(
        functools.partial(_moba_prompt_kernel, n_blocks=n_blocks, scale=HEAD_DIM ** -0.5),
        grid=(batch, groups, n_blocks),
        in_specs=[
            pl.BlockSpec((BLOCK, width), lambda b, hg, c: (b * n_blocks + c, hg)),
            pl.BlockSpec((seq, width), lambda b, hg, c: (b, groups + hg)),
            pl.BlockSpec((seq, width), lambda b, hg, c: (b, 2 * groups + hg)),
            pl.BlockSpec(memory_space=pl.ANY),
        ],
        out_specs=pl.BlockSpec((BLOCK, width), lambda b, hg, c: (b * n_blocks + c, hg)),
        out_shape=jax.ShapeDtypeStruct(base.shape, base.dtype),
        input_output_aliases={3: 0},
        scratch_shapes=[pltpu.VMEM((seq, width), BF16), pltpu.VMEM((width, seq), BF16),
                        pltpu.VMEM((n_blocks, width), F32)],
        compiler_params=_params("parallel", "parallel", "arbitrary"),
        name="moba_prompt",
    )(h, h, h, base)


def _sample_topk_kernel(q_ref, kbar_ref, sel_ref):
    n_past = kbar_ref.shape[0]
    rows = q_ref.shape[0]
    lane_g = lax.broadcasted_iota(jnp.int32, (rows, n_past), 1).astype(F32)
    lane_o = lax.broadcasted_iota(jnp.int32, (rows, HEAD_DIM), 1)
    for hh in range(N_HEADS):
        qh = q_ref[:, hh * HEAD_DIM:(hh + 1) * HEAD_DIM]
        g = _dot_nt(qh, kbar_ref[:, hh, :], precision=HIGHEST)
        out = jnp.zeros((rows, HEAD_DIM), jnp.int32)
        for kk in range(TOP_K):
            best = jnp.max(g, axis=-1, keepdims=True)
            idx = jnp.min(jnp.where(g == best, lane_g, float(n_past)), axis=-1, keepdims=True)
            out = jnp.where(lane_o == kk, idx.astype(jnp.int32), out)
            g = jnp.where(lane_g == idx, -jnp.inf, g)
        sel_ref[hh] = out


def _sample_topk(q8, kbar):
    dec_batch, rows, width = q8.shape
    n_past = kbar.shape[1]
    return pl.pallas_call(
        _sample_topk_kernel,
        grid=(dec_batch,),
        in_specs=[
            pl.BlockSpec((None, rows, width), lambda b: (b, 0, 0)),
            pl.BlockSpec((None, n_past, N_HEADS, HEAD_DIM), lambda b: (b, 0, 0, 0)),
        ],
        out_specs=pl.BlockSpec((None, N_HEADS, rows, HEAD_DIM), lambda b: (b, 0, 0, 0)),
        out_shape=jax.ShapeDtypeStruct((dec_batch, N_HEADS, rows, HEAD_DIM), jnp.int32),
        compiler_params=_params("parallel"),
        name="sample_topk",
    )(q8, kbar)


def _sample_attn_kernel(pt_ref, sel_ref, q_ref, kn_ref, vn_ref, ck_hbm, cv_hbm, o_ref, kbuf, vbuf, sem,
                        *, layer, dec_seq, n_pages, scale):
    b = pl.program_id(0)
    hd = pl.program_id(1)
    step = b * N_HEADS + hd
    n_steps = pl.num_programs(0) * N_HEADS
    slot = step % 2
    sel_per_head = dec_seq * TOP_K
    n_blk = dec_seq * TOP_K

    def page_copies(bb, hh, sl):
        copies = []
        for i in range(n_blk):
            blk = sel_ref[(bb * N_HEADS + hh) * sel_per_head + i]
            for p in range(PAGES_PER_BLOCK):
                page = pt_ref[bb * n_pages + blk * PAGES_PER_BLOCK + p]
                dst = i * PAGES_PER_BLOCK + p
                copies.append(pltpu.make_async_copy(ck_hbm.at[layer, page, :, hh, :], kbuf.at[sl, dst], sem.at[0, sl]))
                copies.append(pltpu.make_async_copy(cv_hbm.at[layer, page, :, hh, :], vbuf.at[sl, dst], sem.at[1, sl]))
        return copies

    @pl.when(step == 0)
    def _():
        for cp in page_copies(b, hd, slot):
            cp.start()

    @pl.when(step + 1 < n_steps)
    def _():
        nxt = step + 1
        for cp in page_copies(nxt // N_HEADS, nxt % N_HEADS, 1 - slot):
            cp.start()

    for cp in page_copies(b, hd, slot):
        cp.wait()

    rows = q_ref.shape[0]
    qb = q_ref[...].astype(BF16)
    s_own_all = _dot_nt(qb, kn_ref[...].astype(BF16)) * scale
    vn = vn_ref[...].astype(BF16)
    col = lax.broadcasted_iota(jnp.int32, (1, rows), 1)

    def block(buf, i):
        return jnp.concatenate([buf[slot, i * PAGES_PER_BLOCK + p] for p in range(PAGES_PER_BLOCK)],
                               axis=0).astype(BF16)

    s_sel = [[_dot_nt(qb, block(kbuf, qi * TOP_K + kk))[qi:qi + 1, :] * scale for kk in range(TOP_K)]
             for qi in range(dec_seq)]
    p_own, p_sel, denoms = [], [], []
    for qi in range(dec_seq):
        s_own = jnp.where(col <= qi, s_own_all[qi:qi + 1, :], NEG)
        m = jnp.max(s_own, axis=-1, keepdims=True)
        for s in s_sel[qi]:
            m = jnp.maximum(m, jnp.max(s, axis=-1, keepdims=True))
        po = jnp.exp(s_own - m)
        ps = [jnp.exp(s - m) for s in s_sel[qi]]
        denom = jnp.sum(po, axis=-1, keepdims=True)
        for p in ps:
            denom = denom + jnp.sum(p, axis=-1, keepdims=True)
        p_own.append(po)
        p_sel.append(ps)
        denoms.append(denom)
    row = lax.broadcasted_iota(jnp.int32, (rows, HEAD_DIM), 0)
    result = jnp.zeros((rows, HEAD_DIM), F32)
    for qi in range(dec_seq):
        out = _dot(jnp.broadcast_to(p_own[qi], (rows, rows)).astype(BF16), vn)
        for kk in range(TOP_K):
            p = jnp.broadcast_to(p_sel[qi][kk], (rows, BLOCK)).astype(BF16)
            out = out + _dot(p, block(vbuf, qi * TOP_K + kk))
        result = jnp.where(row == qi, out / denoms[qi], result)
    o_ref[...] = result


def _sample_attn(q8, k8, v8, cache_k, cache_v, pt_flat, sel_flat, layer, dec_seq, n_pages):
    dec_batch, rows, _ = q8.shape
    n_sel_pages = dec_seq * TOP_K * PAGES_PER_BLOCK
    head_spec = pl.BlockSpec((None, rows, HEAD_DIM), lambda b, hh, pt, sel: (b, 0, hh))
    hbm = pl.BlockSpec(memory_space=pl.ANY)
    return pl.pallas_call(
        functools.partial(_sample_attn_kernel, layer=layer, dec_seq=dec_seq, n_pages=n_pages,
                          scale=HEAD_DIM ** -0.5),
        grid_spec=pltpu.PrefetchScalarGridSpec(
            num_scalar_prefetch=2,
            grid=(dec_batch, N_HEADS),
            in_specs=[head_spec, head_spec, head_spec, hbm, hbm],
            out_specs=head_spec,
            scratch_shapes=[pltpu.VMEM((2, n_sel_pages, PAGE_SIZE, HEAD_DIM), F32),
                            pltpu.VMEM((2, n_sel_pages, PAGE_SIZE, HEAD_DIM), F32),
                            pltpu.SemaphoreType.DMA((2, 2))],
        ),
        out_shape=jax.ShapeDtypeStruct(q8.shape, F32),
        compiler_params=_params("arbitrary", "arbitrary"),
        name="sample_attn",
    )(pt_flat, sel_flat, q8, k8, v8, cache_k, cache_v)


def _s5_tables(a_re, a_im, log_dt, b_re, b_im, c_re, c_im):
    L = SSM_CHUNK
    lam = lax.complex(a_re, a_im)
    dt = jnp.exp(log_dt)[..., None]
    lam_bar = jnp.exp(lam * dt)
    b_bar = ((lam_bar - 1.0) / lam)[..., None] * lax.complex(b_re, b_im)
    c_mat = lax.complex(c_re, c_im)
    pw = [jnp.ones_like(lam_bar)]
    for _ in range(L):
        pw.append(pw[-1] * lam_bar)
    pows = jnp.stack(pw, axis=2)
    z = (c_mat.transpose(0, 1, 3, 2)[:, :, :, None, :]
         * pows.transpose(0, 1, 3, 2)[:, :, :, :, None])
    dd, gg = z.shape[0], z.shape[1]
    z = z.reshape(dd, gg, SSM_STATE, (L + 1) * SSM_GROUP)
    zt = jnp.concatenate([jnp.real(z), -jnp.imag(z)], axis=2)
    bst = jnp.concatenate([jnp.real(b_bar), jnp.imag(b_bar)], axis=2)
    gw = L * SSM_GROUP
    r0 = jnp.einsum('dgpi,dgpn->dgin', bst, zt[..., :gw], precision=HIGHEST)
    tt = jnp.stack([jnp.pad(r0[..., :gw - t * SSM_GROUP], ((0, 0), (0, 0), (0, 0), (t * SSM_GROUP, 0)))
                    for t in range(L)], axis=2).reshape(dd, gg, gw, gw)
    nt = zt[..., SSM_GROUP:]
    mcx = (pows[:, :, L - 1::-1][:, :, :, None, :]
           * b_bar.transpose(0, 1, 3, 2)[:, :, None])
    mcx = mcx.reshape(dd, gg, gw, SSM_STATE)
    mt = jnp.concatenate([jnp.real(mcx), jnp.imag(mcx)], axis=-1)

    def lanes(zc):
        return (jnp.concatenate([jnp.real(zc), jnp.real(zc)], axis=-1),
                jnp.concatenate([-jnp.imag(zc), jnp.imag(zc)], axis=-1))

    steps = []
    cur = pows[:, :, L]
    for _ in range(8):
        steps.append(cur)
        cur = cur * cur
    ar, ai = lanes(jnp.stack(steps, axis=2))
    return dict(tt=tt, mt=mt, nt=nt, ar=ar, ai=ai, pows=pows)


def _chunk_permutation():
    n = SSM_CHUNK * LANES
    src = jnp.arange(n)
    t, g, ch = src // LANES, (src % LANES) // SSM_GROUP, src % SSM_GROUP
    dst = g * (SSM_CHUNK * SSM_GROUP) + t * SSM_GROUP + ch
    return (dst[:, None] == jnp.arange(n)[None, :]).astype(BF16)


def _s5_prompt_kernel(x_ref, perm_ref, tt_ref, mt_ref, nt_ref, ar_ref, ai_ref, d_ref, base_ref, y_ref, hl_ref,
                      *, n_seq, n_chunks):
    del base_ref
    L = SSM_CHUNK
    gw = L * SSM_GROUP
    rows = n_seq * n_chunks
    u_tok = jnp.concatenate([x_ref[:, t, :].astype(BF16) for t in range(L)], axis=1)
    u_grp = _dot(u_tok, perm_ref[...]).astype(BF16)
    kpos = lax.broadcasted_iota(jnp.int32, (rows, 2 * SSM_STATE), 0) % n_chunks
    y_parts = []
    for g in range(GROUPS_PER_LANE_BLOCK):
        ub = u_grp[:, g * gw:(g + 1) * gw]
        w = _dot(ub, mt_ref[g])
        stride, i = 1, 0
        while stride < n_chunks:
            sh = jnp.where(kpos >= stride, pltpu.roll(w, stride, 0), 0.0)
            w = w + sh * ar_ref[g, i:i + 1, :] + pltpu.roll(sh, SSM_STATE, 1) * ai_ref[g, i:i + 1, :]
            stride, i = stride * 2, i + 1
        h_in = jnp.where(kpos >= 1, pltpu.roll(w, 1, 0), 0.0)
        y_parts.append((_dot(ub, tt_ref[g]) + _dot(h_in.astype(BF16), nt_ref[g])).astype(BF16))
        for n in range(n_seq):
            hl_ref[g, n:n + 1, :] = w[(n + 1) * n_chunks - 1:(n + 1) * n_chunks, :]
    y_tok = _dot_nt(jnp.concatenate(y_parts, axis=1), perm_ref[...])
    for t in range(L):
        y_ref[:, t, :] = y_tok[:, t * LANES:(t + 1) * LANES] + d_ref[...] * x_ref[:, t, :]


def _s5_prompt(h3, base3, perm, tt, mt, nt, ar, ai, d_lane, layer, n_seq_total, n_chunks, u_lane_block0,
               seq_per_step=2):
    groups = tt.shape[1]
    lane_blocks = groups // GROUPS_PER_LANE_BLOCK
    rows = seq_per_step * n_chunks
    p2 = 2 * SSM_STATE
    gw = SSM_CHUNK * SSM_GROUP
    gpb = GROUPS_PER_LANE_BLOCK
    n_perm = SSM_CHUNK * LANES
    return pl.pallas_call(
        functools.partial(_s5_prompt_kernel, n_seq=seq_per_step, n_chunks=n_chunks),
        grid=(n_seq_total // seq_per_step, lane_blocks),
        in_specs=[
            pl.BlockSpec((rows, SSM_CHUNK, LANES), lambda s, g: (s, 0, u_lane_block0 + g)),
            pl.BlockSpec((n_perm, n_perm), lambda s, g: (0, 0)),
            pl.BlockSpec((None, gpb, gw, gw), lambda s, g: (layer, g, 0, 0)),
            pl.BlockSpec((None, gpb, gw, p2), lambda s, g: (layer, g, 0, 0)),
            pl.BlockSpec((None, gpb, p2, gw), lambda s, g: (layer, g, 0, 0)),
            pl.BlockSpec((None, gpb, 8, p2), lambda s, g: (layer, g, 0, 0)),
            pl.BlockSpec((None, gpb, 8, p2), lambda s, g: (layer, g, 0, 0)),
            pl.BlockSpec((None, None, 1, LANES), lambda s, g: (layer, g, 0, 0)),
            pl.BlockSpec(memory_space=pl.ANY),
        ],
        out_specs=[
            pl.BlockSpec((rows, SSM_CHUNK, LANES), lambda s, g: (s, 0, g)),
            pl.BlockSpec((None, gpb, seq_per_step, p2), lambda s, g: (s, g, 0, 0)),
        ],
        out_shape=[
            jax.ShapeDtypeStruct(base3.shape, base3.dtype),
            jax.ShapeDtypeStruct((n_seq_total // seq_per_step, groups, seq_per_step, p2), F32),
        ],
        input_output_aliases={8: 0},
        compiler_params=_params("parallel", "arbitrary"),
        name="s5_prompt",
    )(h3, perm, tt, mt, nt, ar, ai, d_lane, base3)


def _s5_sample_kernel(u_ref, h0_ref, tt_ref, mt_ref, nt_ref, ar_ref, ai_ref, d_ref, y_ref, hn_ref):
    groups = u_ref.shape[0]

    def body(g, carry):
        u = u_ref[g]
        h0 = h0_ref[g]
        y_ref[g] = _dot(u, tt_ref[g], HIGHEST) + _dot(h0, nt_ref[g], HIGHEST) + d_ref[g] * u
        hn_ref[g] = (h0 * ar_ref[g] + pltpu.roll(h0, SSM_STATE, 1) * ai_ref[g]
                     + _dot(u, mt_ref[g], HIGHEST))
        return carry

    lax.fori_loop(0, groups, body, 0)


def _s5_sample(u_g, h0_g, tt_s, mt_s, nt_s, ar_s, ai_s, d_s):
    args = (u_g, h0_g, tt_s, mt_s, nt_s, ar_s, ai_s, d_s)
    full = lambda a: pl.BlockSpec(a.shape, lambda i, nd=a.ndim: (0,) * nd)
    return pl.pallas_call(
        _s5_sample_kernel,
        grid=(1,),
        in_specs=[full(a) for a in args],
        out_specs=[full(u_g), full(h0_g)],
        out_shape=[jax.ShapeDtypeStruct(u_g.shape, F32), jax.ShapeDtypeStruct(h0_g.shape, F32)],
        compiler_params=_params("arbitrary"),
        name="s5_sample",
    )(*args)


def _mix_ln_kernel(a_ref, ys_ref, ga_ref, gb_ref, x_ref, wa_ref, wv_ref, wg_ref, wo_ref, g_ref, b_ref,
                   o_ref, zb_ref, acc_ref, *, alpha):
    j = pl.program_id(1)

    @pl.when(j == 0)
    def _():
        zb_ref[...] = jax.nn.gelu(ys_ref[...]).astype(BF16)
        acc_ref[...] = jnp.zeros_like(acc_ref)

    zb = zb_ref[...]
    y_a = _dot(a_ref[...], wa_ref[...])
    y_b = _dot(zb, wv_ref[...]) * jax.nn.sigmoid(_dot(zb, wg_ref[...]))
    mixed = jax.nn.sigmoid(ga_ref[...]) * y_a + jax.nn.sigmoid(gb_ref[...]) * y_b
    acc_ref[...] += _dot(mixed.astype(BF16), wo_ref[...])

    @pl.when(j == pl.num_programs(1) - 1)
    def _():
        o_ref[...] = _layer_norm(alpha * x_ref[...] + acc_ref[...], g_ref[...], b_ref[...])


def _mix_ln(attn, y_ssm, h, x, wa, wv, wg, wo, gain, bias, layer, alpha, tn=512):
    t, d = x.shape
    aw = attn.shape[1]
    sw = y_ssm.shape[1]
    tm = ROW_TILE
    ga_blk = (3 * ATTN_WIDTH + sw) // tn
    gb_blk = ga_blk + d // tn
    return pl.pallas_call(
        functools.partial(_mix_ln_kernel, alpha=alpha),
        grid=(t // tm, d // tn),
        in_specs=[
            pl.BlockSpec((tm, aw), lambda i, j: (i, 0)),
            pl.BlockSpec((tm, sw), lambda i, j: (i, 0)),
            pl.BlockSpec((tm, tn), lambda i, j: (i, ga_blk + j)),
            pl.BlockSpec((tm, tn), lambda i, j: (i, gb_blk + j)),
            pl.BlockSpec((tm, d), lambda i, j: (i, 0)),
            pl.BlockSpec((None, aw, tn), lambda i, j: (layer, 0, j)),
            pl.BlockSpec((None, sw, tn), lambda i, j: (layer, 0, j)),
            pl.BlockSpec((None, sw, tn), lambda i, j: (layer, 0, j)),
            pl.BlockSpec((None, tn, d), lambda i, j: (layer, j, 0)),
            pl.BlockSpec((1, d), lambda i, j: (0, 0)),
            pl.BlockSpec((1, d), lambda i, j: (0, 0)),
        ],
        out_specs=pl.BlockSpec((tm, d), lambda i, j: (i, 0)),
        out_shape=jax.ShapeDtypeStruct((t, d), F32),
        scratch_shapes=[pltpu.VMEM((tm, sw), BF16), pltpu.VMEM((tm, d), F32)],
        compiler_params=_params("parallel", "arbitrary"),
        name="mix_ln",
    )(attn, y_ssm, h, h, x, wa, wv, wg, wo, gain, bias)


def kernel(x_prompt, x_sample, cache_k, cache_v, state_ssm_re, state_ssm_im, page_table, w_in, w_attn_out, w_glu_v, w_glu_g, w_out, ssm_a_re, ssm_a_im, ssm_log_dt, ssm_b_re, ssm_b_im, ssm_c_re, ssm_c_im, ssm_d, ffn1_gate, ffn1_up, ffn1_down, ffn2_gate, ffn2_up, ffn2_down, ln_gain, ln_bias):
    batch, seq, d_model = x_prompt.shape
    dec_batch, dec_seq, _ = x_sample.shape
    depth = w_in.shape[0]
    n_pages = page_table.shape[1]
    past_len = n_pages * PAGE_SIZE
    ssm_width = ssm_d.shape[1]
    groups = ssm_width // SSM_GROUP
    L, CH = SSM_CHUNK, SSM_GROUP
    n_chunks = seq // L
    t_p = batch * seq
    t_s = dec_batch * dec_seq
    t_pad = -(-(t_p + t_s) // DENSE_ROW_TILE) * DENSE_ROW_TILE
    t_tail = t_pad - t_p
    assert past_len % BLOCK == 0 and past_len // BLOCK >= TOP_K
    assert seq % (2 * BLOCK) == 0 and t_pad % L == 0 and batch % 2 == 0
    assert dec_seq <= 8 and dec_seq <= L and n_pages % KBAR_PAGES_PER_STEP == 0
    assert (3 * ATTN_WIDTH) % LANES == 0 and groups % GROUPS_PER_LANE_BLOCK == 0
    alpha = (2.0 * depth) ** 0.25

    x = jnp.concatenate([x_prompt.reshape(t_p, d_model), x_sample.reshape(t_s, d_model),
                         jnp.zeros((t_tail - t_s, d_model), F32)], axis=0)

    pos = jnp.concatenate([jnp.tile(jnp.arange(seq, dtype=jnp.int32), batch),
                           jnp.tile(past_len + jnp.arange(dec_seq, dtype=jnp.int32), dec_batch),
                           jnp.zeros((t_tail - t_s,), jnp.int32)])
    rope_c, rope_s1, rope_s2 = _rope_tables(pos)

    w_ao_b, w_gv_b, w_gg_b, w_out_b = (w.astype(BF16) for w in (w_attn_out, w_glu_v, w_glu_g, w_out))

    tabs = _s5_tables(ssm_a_re, ssm_a_im, ssm_log_dt, ssm_b_re, ssm_b_im, ssm_c_re, ssm_c_im)
    tt_b, mt_b, nt_b = tabs['tt'].astype(BF16), tabs['mt'].astype(BF16), tabs['nt'].astype(BF16)
    perm = _chunk_permutation()
    d_lane = ssm_d.reshape(depth, ssm_width // LANES, 1, LANES)
    d_g = ssm_d.reshape(depth, groups, 1, CH)
    ws = dec_seq * CH
    tt_s = tabs['tt'][:, :, :ws, :ws]
    nt_s = tabs['nt'][:, :, :, :ws]
    mt_s = tabs['mt'][:, :, (L - dec_seq) * CH:, :]
    lam_ds = tabs['pows'][:, :, dec_seq]
    ar_s = jnp.concatenate([jnp.real(lam_ds), jnp.real(lam_ds)], axis=-1)[:, :, None, :]
    ai_s = jnp.concatenate([-jnp.imag(lam_ds), jnp.imag(lam_ds)], axis=-1)[:, :, None, :]
    d_s = jnp.tile(d_g, (1, 1, 1, dec_seq))
    h0_s = jnp.concatenate([state_ssm_re, state_ssm_im], axis=-1).transpose(0, 2, 1, 3)

    pt_flat = page_table.reshape(-1)
    row_pad = 8 - dec_seq

    def pad_rows(a):
        return jnp.pad(a, ((0, 0), (0, row_pad), (0, 0)))

    def tail_base(tail_rows):
        width, dtype = tail_rows.shape[1], tail_rows.dtype
        return jnp.concatenate([jnp.zeros((t_p, width), dtype), tail_rows,
                                jnp.zeros((t_tail - t_s, width), dtype)], axis=0)

    u0 = 3 * ATTN_WIDTH
    hrp_l, hip_l, ks_l, vs_l, hrs_l, his_l = ([] for _ in range(6))
    k_all = jnp.zeros((depth, t_pad, N_HEADS, HEAD_DIM), F32)
    v_all = jnp.zeros((depth, t_pad, N_HEADS, HEAD_DIM), F32)
    for l in range(depth):
        gains = [ln_gain[l, i][None, :] for i in range(3)]
        biases = [ln_bias[l, i][None, :] for i in range(3)]
        x, kbar = _ffn_ln(x, ffn1_gate, ffn1_up, ffn1_down, gains[0], biases[0], l, alpha,
                          kbar_from=(cache_k, pt_flat, dec_batch, n_pages))
        h, k_all, v_all = _in_proj(x, w_in, rope_c, rope_s1, rope_s2, k_all, v_all, l)

        hs = h[t_p:t_p + t_s]
        q_s = hs[:, :ATTN_WIDTH].reshape(dec_batch, dec_seq, ATTN_WIDTH)
        k_s = hs[:, ATTN_WIDTH:2 * ATTN_WIDTH].reshape(dec_batch, dec_seq, ATTN_WIDTH)
        v_s = hs[:, 2 * ATTN_WIDTH:3 * ATTN_WIDTH].reshape(dec_batch, dec_seq, ATTN_WIDTH)
        q8, k8, v8 = pad_rows(q_s), pad_rows(k_s), pad_rows(v_s)
        sel = _sample_topk(q8, kbar)[:, :, :dec_seq, :TOP_K]
        attn_s = _sample_attn(q8, k8, v8, cache_k, cache_v, pt_flat, sel.reshape(-1), l, dec_seq, n_pages)
        attn_s = attn_s[:, :dec_seq].reshape(t_s, ATTN_WIDTH)
        attn = _moba_prompt(h, tail_base(attn_s.astype(BF16)), batch, seq)

        us_g = hs[:, u0:u0 + ssm_width].reshape(dec_batch, dec_seq, groups, CH)
        us_g = us_g.transpose(2, 0, 1, 3).reshape(groups, dec_batch, ws)
        ys_g, hn_g = _s5_sample(us_g, h0_s[l], tt_s[l], mt_s[l], nt_s[l], ar_s[l], ai_s[l], d_s[l])
        y_s = ys_g.reshape(groups, dec_batch, dec_seq, CH).transpose(1, 2, 0, 3).reshape(t_s, ssm_width)
        h3 = h.reshape(t_pad // L, L, h.shape[1])
        y3, hl4 = _s5_prompt(h3, tail_base(y_s).reshape(t_pad // L, L, ssm_width), perm, tt_b, mt_b, nt_b,
                             tabs['ar'], tabs['ai'], d_lane, l, batch, n_chunks, u0 // LANES)
        y_ssm = y3.reshape(t_pad, ssm_width)

        x = _mix_ln(attn, y_ssm, h, x, w_ao_b, w_gv_b, w_gg_b, w_out_b, gains[1], biases[1], l, alpha)
        x = _ffn_ln(x, ffn2_gate, ffn2_up, ffn2_down, gains[2], biases[2], l, alpha)

        hl = hl4.transpose(0, 2, 1, 3).reshape(batch, groups, 2 * SSM_STATE)
        hrp_l.append(hl[..., :SSM_STATE])
        hip_l.append(hl[..., SSM_STATE:])
        ks_l.append(k_s.reshape(dec_batch, dec_seq, N_HEADS, HEAD_DIM))
        vs_l.append(v_s.reshape(dec_batch, dec_seq, N_HEADS, HEAD_DIM))
        hn = hn_g.transpose(1, 0, 2)
        hrs_l.append(hn[..., :SSM_STATE])
        his_l.append(hn[..., SSM_STATE:])

    y_prompt = x[:t_p].reshape(batch, seq, d_model)
    y_sample = x[t_p:t_p + t_s].reshape(dec_batch, dec_seq, d_model)
    return (y_prompt, y_sample,
            k_all[:, :t_p].reshape(depth, batch, seq, N_HEADS, HEAD_DIM),
            v_all[:, :t_p].reshape(depth, batch, seq, N_HEADS, HEAD_DIM),
            jnp.stack(hrp_l), jnp.stack(hip_l),
            jnp.stack(ks_l), jnp.stack(vs_l), jnp.stack(hrs_l), jnp.stack(his_l))
```

```python
import functools

import jax
import jax.numpy as jnp
from jax import lax
from jax.experimental import pallas as pl
from jax.experimental.pallas import tpu as pltpu

N_HEADS = 8
HEAD_DIM = 128
ATTN_WIDTH = N_HEADS * HEAD_DIM
ROT_DIM = HEAD_DIM // 4
ROT_HALF = ROT_DIM // 2
ROPE_THETA = 500000.0
BLOCK = 256
TOP_K = 3
PAGE_SIZE = 128
PAGES_PER_BLOCK = BLOCK // PAGE_SIZE
SSM_GROUP = 16
SSM_STATE = 64
SSM_CHUNK = 16
LANES = 128
GROUPS_PER_LANE_BLOCK = LANES // SSM_GROUP
LN_EPS = 1e-5
NEG = -1e30

ROW_TILE = 528
DENSE_ROW_TILE = 2 * ROW_TILE
VMEM_LIMIT = 56 * 1024 * 1024

F32 = jnp.float32
BF16 = jnp.bfloat16
HIGHEST = lax.Precision.HIGHEST


def _params(*sem):
    return pltpu.CompilerParams(dimension_semantics=sem, vmem_limit_bytes=VMEM_LIMIT)


def _layer_norm(r, g, b):
    mu = jnp.mean(r, axis=-1, keepdims=True)
    d = r - mu
    var = jnp.mean(d * d, axis=-1, keepdims=True)
    return d * lax.rsqrt(var + LN_EPS) * g + b


def _dot(a, b, precision=None):
    return jnp.dot(a, b, precision=precision, preferred_element_type=F32)


def _dot_nt(a, b, precision=None):
    return lax.dot_general(a, b, (((1,), (1,)), ((), ())), precision=precision,
                           preferred_element_type=F32)


KBAR_PAGES_PER_STEP = 8


def _ffn_ln_kernel(*refs, alpha, kbar_steps):
    n_pages = KBAR_PAGES_PER_STEP if kbar_steps else 0
    refs = refs[1:] if kbar_steps else refs
    x_ref, wg_ref, wu_ref, wd_ref, g_ref, b_ref = refs[:6]
    pages = refs[6:6 + n_pages]
    o_ref = refs[6 + n_pages]
    kbar_ref = refs[7 + n_pages] if kbar_steps else None
    xb_ref = refs[-1]
    i = pl.program_id(0)
    j = pl.program_id(1)

    @pl.when(j == 0)
    def _():
        xb_ref[...] = x_ref[...].astype(BF16)
        o_ref[...] = jnp.zeros_like(o_ref)

    xb = xb_ref[...]
    gate = _dot(xb, wg_ref[...].astype(BF16))
    up = _dot(xb, wu_ref[...].astype(BF16))
    hid = (gate * jax.nn.sigmoid(gate) * up).astype(BF16)
    o_ref[...] += _dot(hid, wd_ref[...].astype(BF16))

    if kbar_steps:
        @pl.when(i * pl.num_programs(1) + j < kbar_steps)
        def _():
            for blk in range(n_pages // PAGES_PER_BLOCK):
                tot = None
                for p in range(PAGES_PER_BLOCK):
                    part = jnp.sum(pages[blk * PAGES_PER_BLOCK + p][...], axis=0)
                    tot = part if tot is None else tot + part
                kbar_ref[blk] = tot * (1.0 / BLOCK)

    @pl.when(j == pl.num_programs(1) - 1)
    def _():
        o_ref[...] = _layer_norm(alpha * x_ref[...] + 0.5 * o_ref[...], g_ref[...], b_ref[...])


def _ffn_ln(x, wg, wu, wd, gain, bias, layer, alpha, tf=256, kbar_from=None):
    t, d = x.shape
    f = wg.shape[-1]
    tm = DENSE_ROW_TILE
    grid = (t // tm, f // tf)
    in_specs = [
        pl.BlockSpec((tm, d), lambda i, j, *_: (i, 0), pipeline_mode=pl.Buffered(1)),
        pl.BlockSpec((None, d, tf), lambda i, j, *_: (layer, 0, j)),
        pl.BlockSpec((None, d, tf), lambda i, j, *_: (layer, 0, j)),
        pl.BlockSpec((None, tf, d), lambda i, j, *_: (layer, j, 0)),
        pl.BlockSpec((1, d), lambda i, j, *_: (0, 0)),
        pl.BlockSpec((1, d), lambda i, j, *_: (0, 0)),
    ]
    out_specs = pl.BlockSpec((tm, d), lambda i, j, *_: (i, 0))
    out_shape = jax.ShapeDtypeStruct((t, d), F32)
    scratch = [pltpu.VMEM((tm, d), BF16)]
    if kbar_from is None:
        return pl.pallas_call(
            functools.partial(_ffn_ln_kernel, alpha=alpha, kbar_steps=0),
            grid=grid, in_specs=in_specs, out_specs=out_specs, out_shape=out_shape, scratch_shapes=scratch,
            compiler_params=_params("parallel", "arbitrary"), name="ffn_ln",
        )(x, wg, wu, wd, gain, bias)

    cache_k, pt_flat, dec_batch, n_pages = kbar_from
    pps = KBAR_PAGES_PER_STEP
    kbar_steps = dec_batch * n_pages // pps
    steps_per_seq = n_pages // pps
    blocks_per_step = pps // PAGES_PER_BLOCK
    assert kbar_steps <= grid[0] * grid[1]

    def side_step(i, j):
        return jnp.minimum(i * grid[1] + j, kbar_steps - 1)

    def page_spec(p):
        return pl.BlockSpec((None, None, PAGE_SIZE, N_HEADS, HEAD_DIM),
                            lambda i, j, pt: (layer, pt[side_step(i, j) * pps + p], 0, 0, 0))

    kbar_spec = pl.BlockSpec(
        (None, blocks_per_step, N_HEADS, HEAD_DIM),
        lambda i, j, pt: (side_step(i, j) // steps_per_seq, side_step(i, j) % steps_per_seq, 0, 0))
    out_specs = pl.BlockSpec((tm, d), lambda i, j, *_: (i, 0), pipeline_mode=pl.Buffered(1))
    return pl.pallas_call(
        functools.partial(_ffn_ln_kernel, alpha=alpha, kbar_steps=kbar_steps),
        grid_spec=pltpu.PrefetchScalarGridSpec(
            num_scalar_prefetch=1, grid=grid,
            in_specs=in_specs + [page_spec(p) for p in range(pps)],
            out_specs=[out_specs, kbar_spec], scratch_shapes=scratch),
        out_shape=[out_shape, jax.ShapeDtypeStruct((dec_batch, n_pages // PAGES_PER_BLOCK, N_HEADS, HEAD_DIM), F32)],
        compiler_params=_params("arbitrary", "arbitrary"), name="ffn_ln_kbar",
    )(pt_flat, x, wg, wu, wd, gain, bias, *([cache_k] * pps))


INPROJ_HEADS_PER_TILE = 4
INPROJ_TILES_PER_SECTION = N_HEADS // INPROJ_HEADS_PER_TILE


def _inproj_kernel(x_ref, w_ref, c_ref, s1_ref, s2_ref, o_ref, k3_ref, v3_ref, xb_ref):
    j = pl.program_id(1)
    tps = INPROJ_TILES_PER_SECTION

    @pl.when(j == 0)
    def _():
        xb_ref[...] = x_ref[...].astype(BF16)

    def heads():
        half = INPROJ_HEADS_PER_TILE // 2
        for part in range(2):
            cols = slice(part * half * HEAD_DIM, (part + 1) * half * HEAD_DIM)
            y = _dot(xb_ref[...], w_ref[:, cols].astype(BF16))
            for hl in range(half):
                yield part * half + hl, y[:, hl * HEAD_DIM:(hl + 1) * HEAD_DIM]

    def rope(yh):
        return (yh * c_ref[...]
                + pltpu.roll(yh, ROT_HALF, 1) * s1_ref[...]
                + pltpu.roll(yh, HEAD_DIM - ROT_HALF, 1) * s2_ref[...])

    @pl.when(j < tps)
    def _():
        for hh, yh in heads():
            o_ref[:, hh * HEAD_DIM:(hh + 1) * HEAD_DIM] = rope(yh)

    for tile in range(tps):
        head0 = tile * INPROJ_HEADS_PER_TILE

        @pl.when(j == tps + tile)
        def _():
            for hh, yh in heads():
                kh = rope(yh)
                o_ref[:, hh * HEAD_DIM:(hh + 1) * HEAD_DIM] = kh
                k3_ref[:, head0 + hh, :] = kh

        @pl.when(j == 2 * tps + tile)
        def _():
            for hh, yh in heads():
                o_ref[:, hh * HEAD_DIM:(hh + 1) * HEAD_DIM] = yh
                v3_ref[:, head0 + hh, :] = yh

    @pl.when(j >= 3 * tps)
    def _():
        o_ref[...] = _dot(xb_ref[...], w_ref[...].astype(BF16))


def _in_proj(x, w_in, rope_c, rope_s1, rope_s2, k_all, v_all, layer):
    t, d = x.shape
    n = w_in.shape[-1]
    tm, tn = DENSE_ROW_TILE, INPROJ_HEADS_PER_TILE * HEAD_DIM
    kv_spec = pl.BlockSpec((None, tm, N_HEADS, HEAD_DIM), lambda i, j: (layer, i, 0, 0))
    hbm = pl.BlockSpec(memory_space=pl.ANY)

    def body(x_ref, w_ref, c_ref, s1_ref, s2_ref, k_in, v_in, o_ref, k3_ref, v3_ref, xb_ref):
        del k_in, v_in
        _inproj_kernel(x_ref, w_ref, c_ref, s1_ref, s2_ref, o_ref, k3_ref, v3_ref, xb_ref)

    return pl.pallas_call(
        body,
        grid=(t // tm, n // tn),
        in_specs=[
            pl.BlockSpec((tm, d), lambda i, j: (i, 0), pipeline_mode=pl.Buffered(1)),
            pl.BlockSpec((None, d, tn), lambda i, j: (layer, 0, j)),
            pl.BlockSpec((tm, HEAD_DIM), lambda i, j: (i, 0)),
            pl.BlockSpec((tm, HEAD_DIM), lambda i, j: (i, 0)),
            pl.BlockSpec((tm, HEAD_DIM), lambda i, j: (i, 0)),
            hbm, hbm,
        ],
        out_specs=[pl.BlockSpec((tm, tn), lambda i, j: (i, j)), kv_spec, kv_spec],
        out_shape=[jax.ShapeDtypeStruct((t, n), F32),
                   jax.ShapeDtypeStruct(k_all.shape, F32), jax.ShapeDtypeStruct(v_all.shape, F32)],
        input_output_aliases={5: 1, 6: 2},
        scratch_shapes=[pltpu.VMEM((tm, d), BF16)],
        compiler_params=_params("parallel", "arbitrary"),
        name="in_proj_rope",
    )(x, w_in, rope_c, rope_s1, rope_s2, k_all, v_all)


def _rope_tables(pos):
    inv = jnp.power(ROPE_THETA, -jnp.arange(ROT_HALF, dtype=F32) * 2.0 / ROT_DIM)
    ang = pos.astype(F32)[:, None] * inv[None, :]
    cos, sin = jnp.cos(ang), jnp.sin(ang)
    t = pos.shape[0]
    ones = jnp.ones((t, HEAD_DIM - ROT_DIM), F32)
    zeros_h = jnp.zeros((t, ROT_HALF), F32)
    zeros_r = jnp.zeros((t, HEAD_DIM - ROT_DIM), F32)
    c = jnp.concatenate([cos, cos, ones], axis=1)
    s1 = jnp.concatenate([zeros_h, sin, zeros_r], axis=1)
    s2 = jnp.concatenate([-sin, zeros_h, zeros_r], axis=1)
    return c, s1, s2


MOBA_HEADS_PER_STEP = 8


def _moba_prompt_kernel(q_ref, k_ref, v_ref, base_ref, o_ref, kb_ref, vt_ref, kbar_ref, *, n_blocks, scale):
    del base_ref
    c = pl.program_id(2)
    lanes_of = lambda hd: slice(hd * HEAD_DIM, (hd + 1) * HEAD_DIM)

    @pl.when(c == 0)
    def _():
        for n in range(n_blocks):
            kbar_ref[n:n + 1, :] = jnp.mean(k_ref[n * BLOCK:(n + 1) * BLOCK, :], axis=0, keepdims=True)
        kb_ref[...] = k_ref[...].astype(BF16)
        vt_ref[...] = v_ref[...].T.astype(BF16)

    blk_id = lax.broadcasted_iota(jnp.int32, (n_blocks, BLOCK), 0)
    key_row = lax.broadcasted_iota(jnp.int32, (BLOCK, BLOCK), 0)
    qry_col = lax.broadcasted_iota(jnp.int32, (BLOCK, BLOCK), 1)
    causal = key_row <= qry_col

    def pick_blocks(q, kbar):
        gates = _dot_nt(kbar, q, precision=HIGHEST)
        rank = jnp.zeros((n_blocks, BLOCK), F32)
        for m in range(n_blocks):
            gm = gates[m:m + 1, :]
            past = jnp.where(m < c, 1.0, 0.0)
            ge = jnp.where(gm >= gates, past, 0.0)
            gt = jnp.where(gm > gates, past, 0.0)
            rank = rank + jnp.where(blk_id > m, ge, gt)
        return jnp.where(rank < TOP_K, jnp.where(blk_id < c, 1.0, 0.0), 0.0)

    def attend_head(hd, sel, qb, nb, first_own):
        outs = []
        for qs in range(BLOCK // LANES):
            q_rows = slice(qs * LANES, (qs + 1) * LANES)
            qb_s = qb[q_rows, :]
            m = l = acc = None
            for n in range(nb):
                s = _dot_nt(kb_ref[n * BLOCK:(n + 1) * BLOCK, lanes_of(hd)], qb_s)
                keep = sel[n:n + 1, q_rows]
                if n >= first_own:
                    keep = keep + jnp.where(causal[:, q_rows], jnp.where(c == n, 1.0, 0.0), 0.0)
                s = jnp.where(keep > 0.5, s, NEG)
                bm = s.max(axis=0, keepdims=True)
                vt_n = vt_ref[lanes_of(hd), n * BLOCK:(n + 1) * BLOCK]
                if m is None:
                    m, p = bm, jnp.exp(s - bm)
                    l, acc = p.sum(axis=0, keepdims=True), _dot(vt_n, p.astype(BF16))
                else:
                    m_new = jnp.maximum(m, bm)
                    a, p = jnp.exp(m - m_new), jnp.exp(s - m_new)
                    l = a * l + p.sum(axis=0, keepdims=True)
                    acc = a * acc + _dot(vt_n, p.astype(BF16))
                    m = m_new
            outs.append((acc * (1.0 / l)).T)
        return jnp.concatenate(outs, axis=0).astype(o_ref.dtype)

    heads = range(MOBA_HEADS_PER_STEP)
    qs = [q_ref[:, lanes_of(hd)] for hd in heads]
    sels = [pick_blocks(qs[hd], kbar_ref[:, lanes_of(hd)]) for hd in heads]
    qbs = [(q * scale).astype(BF16) for q in qs]

    def attend(nb, first_own):
        o_ref[...] = jnp.concatenate([attend_head(hd, sels[hd], qbs[hd], nb, first_own) for hd in heads], axis=1)

    step = 2
    for nb in range(step, n_blocks + 1, step):
        pl.when(jnp.logical_and(c < nb, c >= nb - step))(functools.partial(attend, nb, nb - step))


def _moba_prompt(h, base, batch, seq):
    n_blocks = seq // BLOCK
    width = MOBA_HEADS_PER_STEP * HEAD_DIM
    groups = N_HEADS // MOBA_HEADS_PER_STEP
    return pl.pallas_call(
        functools.partial(_moba_prompt_kernel, n_blocks=n_blocks, scale=HEAD_DIM ** -0.5),
        grid=(batch, groups, n_blocks),
        in_specs=[
            pl.BlockSpec((BLOCK, width), lambda b, hg, c: (b * n_blocks + c, hg)),
            pl.BlockSpec((seq, width), lambda b, hg, c: (b, groups + hg)),
            pl.BlockSpec((seq, width), lambda b, hg, c: (b, 2 * groups + hg)),
            pl.BlockSpec(memory_space=pl.ANY),
        ],
        out_specs=pl.BlockSpec((BLOCK, width), lambda b, hg, c: (b * n_blocks + c, hg)),
        out_shape=jax.ShapeDtypeStruct(base.shape, base.dtype),
        input_output_aliases={3: 0},
        scratch_shapes=[pltpu.VMEM((seq, width), BF16), pltpu.VMEM((width, seq), BF16),
                        pltpu.VMEM((n_blocks, width), F32)],
        compiler_params=_params("parallel", "parallel", "arbitrary"),
        name="moba_prompt",
    )(h, h, h, base)


def _sample_topk_kernel(q_ref, kbar_ref, sel_ref):
    n_past = kbar_ref.shape[0]
    rows = q_ref.shape[0]
    lane_g = lax.broadcasted_iota(jnp.int32, (rows, n_past), 1).astype(F32)
    lane_o = lax.broadcasted_iota(jnp.int32, (rows, HEAD_DIM), 1)
    for hh in range(N_HEADS):
        qh = q_ref[:, hh * HEAD_DIM:(hh + 1) * HEAD_DIM]
        g = _dot_nt(qh, kbar_ref[:, hh, :], precision=HIGHEST)
        out = jnp.zeros((rows, HEAD_DIM), jnp.int32)
        for kk in range(TOP_K):
            best = jnp.max(g, axis=-1, keepdims=True)
            idx = jnp.min(jnp.where(g == best, lane_g, float(n_past)), axis=-1, keepdims=True)
            out = jnp.where(lane_o == kk, idx.astype(jnp.int32), out)
            g = jnp.where(lane_g == idx, -jnp.inf, g)
        sel_ref[hh] = out


def _sample_topk(q8, kbar):
    dec_batch, rows, width = q8.shape
    n_past = kbar.shape[1]
    return pl.pallas_call(
        _sample_topk_kernel,
        grid=(dec_batch,),
        in_specs=[
            pl.BlockSpec((None, rows, width), lambda b: (b, 0, 0)),
            pl.BlockSpec((None, n_past, N_HEADS, HEAD_DIM), lambda b: (b, 0, 0, 0)),
        ],
        out_specs=pl.BlockSpec((None, N_HEADS, rows, HEAD_DIM), lambda b: (b, 0, 0, 0)),
        out_shape=jax.ShapeDtypeStruct((dec_batch, N_HEADS, rows, HEAD_DIM), jnp.int32),
        compiler_params=_params("parallel"),
        name="sample_topk",
    )(q8, kbar)


def _sample_attn_kernel(pt_ref, sel_ref, q_ref, kn_ref, vn_ref, ck_hbm, cv_hbm, o_ref, kbuf, vbuf, sem,
                        *, layer, dec_seq, n_pages, scale):
    b = pl.program_id(0)
    hd = pl.program_id(1)
    step = b * N_HEADS + hd
    n_steps = pl.num_programs(0) * N_HEADS
    slot = step % 2
    sel_per_head = dec_seq * TOP_K
    n_blk = dec_seq * TOP_K

    def page_copies(bb, hh, sl):
        copies = []
        for i in range(n_blk):
            blk = sel_ref[(bb * N_HEADS + hh) * sel_per_head + i]
            for p in range(PAGES_PER_BLOCK):
                page = pt_ref[bb * n_pages + blk * PAGES_PER_BLOCK + p]
                dst = i * PAGES_PER_BLOCK + p
                copies.append(pltpu.make_async_copy(ck_hbm.at[layer, page, :, hh, :], kbuf.at[sl, dst], sem.at[0, sl]))
                copies.append(pltpu.make_async_copy(cv_hbm.at[layer, page, :, hh, :], vbuf.at[sl, dst], sem.at[1, sl]))
        return copies

    @pl.when(step == 0)
    def _():
        for cp in page_copies(b, hd, slot):
            cp.start()

    @pl.when(step + 1 < n_steps)
    def _():
        nxt = step + 1
        for cp in page_copies(nxt // N_HEADS, nxt % N_HEADS, 1 - slot):
            cp.start()

    for cp in page_copies(b, hd, slot):
        cp.wait()

    rows = q_ref.shape[0]
    qb = q_ref[...].astype(BF16)
    s_own_all = _dot_nt(qb, kn_ref[...].astype(BF16)) * scale
    vn = vn_ref[...].astype(BF16)
    col = lax.broadcasted_iota(jnp.int32, (1, rows), 1)

    def block(buf, i):
        return jnp.concatenate([buf[slot, i * PAGES_PER_BLOCK + p] for p in range(PAGES_PER_BLOCK)],
                               axis=0).astype(BF16)

    s_sel = [[_dot_nt(qb, block(kbuf, qi * TOP_K + kk))[qi:qi + 1, :] * scale for kk in range(TOP_K)]
             for qi in range(dec_seq)]
    p_own, p_sel, denoms = [], [], []
    for qi in range(dec_seq):
        s_own = jnp.where(col <= qi, s_own_all[qi:qi + 1, :], NEG)
        m = jnp.max(s_own, axis=-1, keepdims=True)
        for s in s_sel[qi]:
            m = jnp.maximum(m, jnp.max(s, axis=-1, keepdims=True))
        po = jnp.exp(s_own - m)
        ps = [jnp.exp(s - m) for s in s_sel[qi]]
        denom = jnp.sum(po, axis=-1, keepdims=True)
        for p in ps:
            denom = denom + jnp.sum(p, axis=-1, keepdims=True)
        p_own.append(po)
        p_sel.append(ps)
        denoms.append(denom)
    row = lax.broadcasted_iota(jnp.int32, (rows, HEAD_DIM), 0)
    result = jnp.zeros((rows, HEAD_DIM), F32)
    for qi in range(dec_seq):
        out = _dot(jnp.broadcast_to(p_own[qi], (rows, rows)).astype(BF16), vn)
        for kk in range(TOP_K):
            p = jnp.broadcast_to(p_sel[qi][kk], (rows, BLOCK)).astype(BF16)
            out = out + _dot(p, block(vbuf, qi * TOP_K + kk))
        result = jnp.where(row == qi, out / denoms[qi], result)
    o_ref[...] = result


def _sample_attn(q8, k8, v8, cache_k, cache_v, pt_flat, sel_flat, layer, dec_seq, n_pages):
    dec_batch, rows, _ = q8.shape
    n_sel_pages = dec_seq * TOP_K * PAGES_PER_BLOCK
    head_spec = pl.BlockSpec((None, rows, HEAD_DIM), lambda b, hh, pt, sel: (b, 0, hh))
    hbm = pl.BlockSpec(memory_space=pl.ANY)
    return pl.pallas_call(
        functools.partial(_sample_attn_kernel, layer=layer, dec_seq=dec_seq, n_pages=n_pages,
                          scale=HEAD_DIM ** -0.5),
        grid_spec=pltpu.PrefetchScalarGridSpec(
            num_scalar_prefetch=2,
            grid=(dec_batch, N_HEADS),
            in_specs=[head_spec, head_spec, head_spec, hbm, hbm],
            out_specs=head_spec,
            scratch_shapes=[pltpu.VMEM((2, n_sel_pages, PAGE_SIZE, HEAD_DIM), F32),
                            pltpu.VMEM((2, n_sel_pages, PAGE_SIZE, HEAD_DIM), F32),
                            pltpu.SemaphoreType.DMA((2, 2))],
        ),
        out_shape=jax.ShapeDtypeStruct(q8.shape, F32),
        compiler_params=_params("arbitrary", "arbitrary"),
        name="sample_attn",
    )(pt_flat, sel_flat, q8, k8, v8, cache_k, cache_v)


def _s5_tables(a_re, a_im, log_dt, b_re, b_im, c_re, c_im):
    L = SSM_CHUNK
    lam = lax.complex(a_re, a_im)
    dt = jnp.exp(log_dt)[..., None]
    lam_bar = jnp.exp(lam * dt)
    b_bar = ((lam_bar - 1.0) / lam)[..., None] * lax.complex(b_re, b_im)
    c_mat = lax.complex(c_re, c_im)
    pw = [jnp.ones_like(lam_bar)]
    for _ in range(L):
        pw.append(pw[-1] * lam_bar)
    pows = jnp.stack(pw, axis=2)
    z = (c_mat.transpose(0, 1, 3, 2)[:, :, :, None, :]
         * pows.transpose(0, 1, 3, 2)[:, :, :, :, None])
    dd, gg = z.shape[0], z.shape[1]
    z = z.reshape(dd, gg, SSM_STATE, (L + 1) * SSM_GROUP)
    zt = jnp.concatenate([jnp.real(z), -jnp.imag(z)], axis=2)
    bst = jnp.concatenate([jnp.real(b_bar), jnp.imag(b_bar)], axis=2)
    gw = L * SSM_GROUP
    r0 = jnp.einsum('dgpi,dgpn->dgin', bst, zt[..., :gw], precision=HIGHEST)
    tt = jnp.stack([jnp.pad(r0[..., :gw - t * SSM_GROUP], ((0, 0), (0, 0), (0, 0), (t * SSM_GROUP, 0)))
                    for t in range(L)], axis=2).reshape(dd, gg, gw, gw)
    nt = zt[..., SSM_GROUP:]
    mcx = (pows[:, :, L - 1::-1][:, :, :, None, :]
           * b_bar.transpose(0, 1, 3, 2)[:, :, None])
    mcx = mcx.reshape(dd, gg, gw, SSM_STATE)
    mt = jnp.concatenate([jnp.real(mcx), jnp.imag(mcx)], axis=-1)

    def lanes(zc):
        return (jnp.concatenate([jnp.real(zc), jnp.real(zc)], axis=-1),
                jnp.concatenate([-jnp.imag(zc), jnp.imag(zc)], axis=-1))

    steps = []
    cur = pows[:, :, L]
    for _ in range(8):
        steps.append(cur)
        cur = cur * cur
    ar, ai = lanes(jnp.stack(steps, axis=2))
    return dict(tt=tt, mt=mt, nt=nt, ar=ar, ai=ai, pows=pows)


def _chunk_permutation():
    n = SSM_CHUNK * LANES
    src = jnp.arange(n)
    t, g, ch = src // LANES, (src % LANES) // SSM_GROUP, src % SSM_GROUP
    dst = g * (SSM_CHUNK * SSM_GROUP) + t * SSM_GROUP + ch
    return (dst[:, None] == jnp.arange(n)[None, :]).astype(BF16)


def _s5_prompt_kernel(x_ref, perm_ref, tt_ref, mt_ref, nt_ref, ar_ref, ai_ref, d_ref, base_ref, y_ref, hl_ref,
                      *, n_seq, n_chunks):
    del base_ref
    L = SSM_CHUNK
    gw = L * SSM_GROUP
    rows = n_seq * n_chunks
    u_tok = jnp.concatenate([x_ref[:, t, :].astype(BF16) for t in range(L)], axis=1)
    u_grp = _dot(u_tok, perm_ref[...]).astype(BF16)
    kpos = lax.broadcasted_iota(jnp.int32, (rows, 2 * SSM_STATE), 0) % n_chunks
    y_parts = []
    for g in range(GROUPS_PER_LANE_BLOCK):
        ub = u_grp[:, g * gw:(g + 1) * gw]
        w = _dot(ub, mt_ref[g])
        stride, i = 1, 0
        while stride < n_chunks:
            sh = jnp.where(kpos >= stride, pltpu.roll(w, stride, 0), 0.0)
            w = w + sh * ar_ref[g, i:i + 1, :] + pltpu.roll(sh, SSM_STATE, 1) * ai_ref[g, i:i + 1, :]
            stride, i = stride * 2, i + 1
        h_in = jnp.where(kpos >= 1, pltpu.roll(w, 1, 0), 0.0)
        y_parts.append((_dot(ub, tt_ref[g]) + _dot(h_in.astype(BF16), nt_ref[g])).astype(BF16))
        for n in range(n_seq):
            hl_ref[g, n:n + 1, :] = w[(n + 1) * n_chunks - 1:(n + 1) * n_chunks, :]
    y_tok = _dot_nt(jnp.concatenate(y_parts, axis=1), perm_ref[...])
    for t in range(L):
        y_ref[:, t, :] = y_tok[:, t * LANES:(t + 1) * LANES] + d_ref[...] * x_ref[:, t, :]


def _s5_prompt(h3, base3, perm, tt, mt, nt, ar, ai, d_lane, layer, n_seq_total, n_chunks, u_lane_block0,
               seq_per_step=4):
    groups = tt.shape[1]
    lane_blocks = groups // GROUPS_PER_LANE_BLOCK
    rows = seq_per_step * n_chunks
    p2 = 2 * SSM_STATE
    gw = SSM_CHUNK * SSM_GROUP
    gpb = GROUPS_PER_LANE_BLOCK
    n_perm = SSM_CHUNK * LANES
    return pl.pallas_call(
        functools.partial(_s5_prompt_kernel, n_seq=seq_per_step, n_chunks=n_chunks),
        grid=(n_seq_total // seq_per_step, lane_blocks),
        in_specs=[
            pl.BlockSpec((rows, SSM_CHUNK, LANES), lambda s, g: (s, 0, u_lane_block0 + g)),
            pl.BlockSpec((n_perm, n_perm), lambda s, g: (0, 0)),
            pl.BlockSpec((None, gpb, gw, gw), lambda s, g: (layer, g, 0, 0)),
            pl.BlockSpec((None, gpb, gw, p2), lambda s, g: (layer, g, 0, 0)),
            pl.BlockSpec((None, gpb, p2, gw), lambda s, g: (layer, g, 0, 0)),
            pl.BlockSpec((None, gpb, 8, p2), lambda s, g: (layer, g, 0, 0)),
            pl.BlockSpec((None, gpb, 8, p2), lambda s, g: (layer, g, 0, 0)),
            pl.BlockSpec((None, None, 1, LANES), lambda s, g: (layer, g, 0, 0)),
            pl.BlockSpec(memory_space=pl.ANY),
        ],
        out_specs=[
            pl.BlockSpec((rows, SSM_CHUNK, LANES), lambda s, g: (s, 0, g)),
            pl.BlockSpec((None, gpb, seq_per_step, p2), lambda s, g: (s, g, 0, 0)),
        ],
        out_shape=[
            jax.ShapeDtypeStruct(base3.shape, base3.dtype),
            jax.ShapeDtypeStruct((n_seq_total // seq_per_step, groups, seq_per_step, p2), F32),
        ],
        input_output_aliases={8: 0},
        compiler_params=_params("parallel", "arbitrary"),
        name="s5_prompt",
    )(h3, perm, tt, mt, nt, ar, ai, d_lane, base3)


def _s5_sample_kernel(u_ref, h0_ref, tt_ref, mt_ref, nt_ref, ar_ref, ai_ref, d_ref, y_ref, hn_ref):
    groups = u_ref.shape[0]

    def body(g, carry):
        u = u_ref[g]
        h0 = h0_ref[g]
        y_ref[g] = _dot(u, tt_ref[g], HIGHEST) + _dot(h0, nt_ref[g], HIGHEST) + d_ref[g] * u
        hn_ref[g] = (h0 * ar_ref[g] + pltpu.roll(h0, SSM_STATE, 1) * ai_ref[g]
                     + _dot(u, mt_ref[g], HIGHEST))
        return carry

    lax.fori_loop(0, groups, body, 0)


def _s5_sample(u_g, h0_g, tt_s, mt_s, nt_s, ar_s, ai_s, d_s):
    args = (u_g, h0_g, tt_s, mt_s, nt_s, ar_s, ai_s, d_s)
    full = lambda a: pl.BlockSpec(a.shape, lambda i, nd=a.ndim: (0,) * nd)
    return pl.pallas_call(
        _s5_sample_kernel,
        grid=(1,),
        in_specs=[full(a) for a in args],
        out_specs=[full(u_g), full(h0_g)],
        out_shape=[jax.ShapeDtypeStruct(u_g.shape, F32), jax.ShapeDtypeStruct(h0_g.shape, F32)],
        compiler_params=_params("arbitrary"),
        name="s5_sample",
    )(*args)


def _mix_ln_kernel(a_ref, ys_ref, ga_ref, gb_ref, x_ref, wa_ref, wv_ref, wg_ref, wo_ref, g_ref, b_ref,
                   o_ref, zb_ref, acc_ref, *, alpha):
    j = pl.program_id(1)

    @pl.when(j == 0)
    def _():
        zb_ref[...] = jax.nn.gelu(ys_ref[...]).astype(BF16)
        acc_ref[...] = jnp.zeros_like(acc_ref)

    zb = zb_ref[...]
    y_a = _dot(a_ref[...], wa_ref[...])
    y_b = _dot(zb, wv_ref[...]) * jax.nn.sigmoid(_dot(zb, wg_ref[...]))
    mixed = jax.nn.sigmoid(ga_ref[...]) * y_a + jax.nn.sigmoid(gb_ref[...]) * y_b
    acc_ref[...] += _dot(mixed.astype(BF16), wo_ref[...])

    @pl.when(j == pl.num_programs(1) - 1)
    def _():
        o_ref[...] = _layer_norm(alpha * x_ref[...] + acc_ref[...], g_ref[...], b_ref[...])


def _mix_ln(attn, y_ssm, h, x, wa, wv, wg, wo, gain, bias, layer, alpha, tn=512):
    t, d = x.shape
    aw = attn.shape[1]
    sw = y_ssm.shape[1]
    tm = ROW_TILE
    ga_blk = (3 * ATTN_WIDTH + sw) // tn
    gb_blk = ga_blk + d // tn
    return pl.pallas_call(
        functools.partial(_mix_ln_kernel, alpha=alpha),
        grid=(t // tm, d // tn),
        in_specs=[
            pl.BlockSpec((tm, aw), lambda i, j: (i, 0)),
            pl.BlockSpec((tm, sw), lambda i, j: (i, 0)),
            pl.BlockSpec((tm, tn), lambda i, j: (i, ga_blk + j)),
            pl.BlockSpec((tm, tn), lambda i, j: (i, gb_blk + j)),
            pl.BlockSpec((tm, d), lambda i, j: (i, 0)),
            pl.BlockSpec((None, aw, tn), lambda i, j: (layer, 0, j)),
            pl.BlockSpec((None, sw, tn), lambda i, j: (layer, 0, j)),
            pl.BlockSpec((None, sw, tn), lambda i, j: (layer, 0, j)),
            pl.BlockSpec((None, tn, d), lambda i, j: (layer, j, 0)),
            pl.BlockSpec((1, d), lambda i, j: (0, 0)),
            pl.BlockSpec((1, d), lambda i, j: (0, 0)),
        ],
        out_specs=pl.BlockSpec((tm, d), lambda i, j: (i, 0)),
        out_shape=jax.ShapeDtypeStruct((t, d), F32),
        scratch_shapes=[pltpu.VMEM((tm, sw), BF16), pltpu.VMEM((tm, d), F32)],
        compiler_params=_params("parallel", "arbitrary"),
        name="mix_ln",
    )(attn, y_ssm, h, h, x, wa, wv, wg, wo, gain, bias)


def kernel(x_prompt, x_sample, cache_k, cache_v, state_ssm_re, state_ssm_im, page_table, w_in, w_attn_out, w_glu_v, w_glu_g, w_out, ssm_a_re, ssm_a_im, ssm_log_dt, ssm_b_re, ssm_b_im, ssm_c_re, ssm_c_im, ssm_d, ffn1_gate, ffn1_up, ffn1_down, ffn2_gate, ffn2_up, ffn2_down, ln_gain, ln_bias):
    batch, seq, d_model = x_prompt.shape
    dec_batch, dec_seq, _ = x_sample.shape
    depth = w_in.shape[0]
    n_pages = page_table.shape[1]
    past_len = n_pages * PAGE_SIZE
    ssm_width = ssm_d.shape[1]
    groups = ssm_width // SSM_GROUP
    L, CH = SSM_CHUNK, SSM_GROUP
    n_chunks = seq // L
    t_p = batch * seq
    t_s = dec_batch * dec_seq
    t_pad = -(-(t_p + t_s) // DENSE_ROW_TILE) * DENSE_ROW_TILE
    t_tail = t_pad - t_p
    assert past_len % BLOCK == 0 and past_len // BLOCK >= TOP_K
    assert seq % (2 * BLOCK) == 0 and t_pad % L == 0 and batch % 4 == 0
    assert dec_seq <= 8 and dec_seq <= L and n_pages % KBAR_PAGES_PER_STEP == 0
    assert (3 * ATTN_WIDTH) % LANES == 0 and groups % GROUPS_PER_LANE_BLOCK == 0
    alpha = (2.0 * depth) ** 0.25

    x = jnp.concatenate([x_prompt.reshape(t_p, d_model), x_sample.reshape(t_s, d_model),
                         jnp.zeros((t_tail - t_s, d_model), F32)], axis=0)

    pos = jnp.concatenate([jnp.tile(jnp.arange(seq, dtype=jnp.int32), batch),
                           jnp.tile(past_len + jnp.arange(dec_seq, dtype=jnp.int32), dec_batch),
                           jnp.zeros((t_tail - t_s,), jnp.int32)])
    rope_c, rope_s1, rope_s2 = _rope_tables(pos)

    w_ao_b, w_gv_b, w_gg_b, w_out_b = (w.astype(BF16) for w in (w_attn_out, w_glu_v, w_glu_g, w_out))

    tabs = _s5_tables(ssm_a_re, ssm_a_im, ssm_log_dt, ssm_b_re, ssm_b_im, ssm_c_re, ssm_c_im)
    tt_b, mt_b, nt_b = tabs['tt'].astype(BF16), tabs['mt'].astype(BF16), tabs['nt'].astype(BF16)
    perm = _chunk_permutation()
    d_lane = ssm_d.reshape(depth, ssm_width // LANES, 1, LANES)
    d_g = ssm_d.reshape(depth, groups, 1, CH)
    ws = dec_seq * CH
    tt_s = tabs['tt'][:, :, :ws, :ws]
    nt_s = tabs['nt'][:, :, :, :ws]
    mt_s = tabs['mt'][:, :, (L - dec_seq) * CH:, :]
    lam_ds = tabs['pows'][:, :, dec_seq]
    ar_s = jnp.concatenate([jnp.real(lam_ds), jnp.real(lam_ds)], axis=-1)[:, :, None, :]
    ai_s = jnp.concatenate([-jnp.imag(lam_ds), jnp.imag(lam_ds)], axis=-1)[:, :, None, :]
    d_s = jnp.tile(d_g, (1, 1, 1, dec_seq))
    h0_s = jnp.concatenate([state_ssm_re, state_ssm_im], axis=-1).transpose(0, 2, 1, 3)

    pt_flat = page_table.reshape(-1)
    row_pad = 8 - dec_seq

    def pad_rows(a):
        return jnp.pad(a, ((0, 0), (0, row_pad), (0, 0)))

    def tail_base(tail_rows):
        width, dtype = tail_rows.shape[1], tail_rows.dtype
        return jnp.concatenate([jnp.zeros((t_p, width), dtype), tail_rows,
                                jnp.zeros((t_tail - t_s, width), dtype)], axis=0)

    u0 = 3 * ATTN_WIDTH
    hrp_l, hip_l, ks_l, vs_l, hrs_l, his_l = ([] for _ in range(6))
    k_all = jnp.zeros((depth, t_pad, N_HEADS, HEAD_DIM), F32)
    v_all = jnp.zeros((depth, t_pad, N_HEADS, HEAD_DIM), F32)
    for l in range(depth):
        gains = [ln_gain[l, i][None, :] for i in range(3)]
        biases = [ln_bias[l, i][None, :] for i in range(3)]
        x, kbar = _ffn_ln(x, ffn1_gate, ffn1_up, ffn1_down, gains[0], biases[0], l, alpha,
                          kbar_from=(cache_k, pt_flat, dec_batch, n_pages))
        h, k_all, v_all = _in_proj(x, w_in, rope_c, rope_s1, rope_s2, k_all, v_all, l)

        hs = h[t_p:t_p + t_s]
        q_s = hs[:, :ATTN_WIDTH].reshape(dec_batch, dec_seq, ATTN_WIDTH)
        k_s = hs[:, ATTN_WIDTH:2 * ATTN_WIDTH].reshape(dec_batch, dec_seq, ATTN_WIDTH)
        v_s = hs[:, 2 * ATTN_WIDTH:3 * ATTN_WIDTH].reshape(dec_batch, dec_seq, ATTN_WIDTH)
        q8, k8, v8 = pad_rows(q_s), pad_rows(k_s), pad_rows(v_s)
        sel = _sample_topk(q8, kbar)[:, :, :dec_seq, :TOP_K]
        attn_s = _sample_attn(q8, k8, v8, cache_k, cache_v, pt_flat, sel.reshape(-1), l, dec_seq, n_pages)
        attn_s = attn_s[:, :dec_seq].reshape(t_s, ATTN_WIDTH)
        attn = _moba_prompt(h, tail_base(attn_s.astype(BF16)), batch, seq)

        us_g = hs[:, u0:u0 + ssm_width].reshape(dec_batch, dec_seq, groups, CH)
        us_g = us_g.transpose(2, 0, 1, 3).reshape(groups, dec_batch, ws)
        ys_g, hn_g = _s5_sample(us_g, h0_s[l], tt_s[l], mt_s[l], nt_s[l], ar_s[l], ai_s[l], d_s[l])
        y_s = ys_g.reshape(groups, dec_batch, dec_seq, CH).transpose(1, 2, 0, 3).reshape(t_s, ssm_width)
        h3 = h.reshape(t_pad // L, L, h.shape[1])
        y3, hl4 = _s5_prompt(h3, tail_base(y_s).reshape(t_pad // L, L, ssm_width), perm, tt_b, mt_b, nt_b,
                             tabs['ar'], tabs['ai'], d_lane, l, batch, n_chunks, u0 // LANES)
        y_ssm = y3.reshape(t_pad, ssm_width)

        x = _mix_ln(attn, y_ssm, h, x, w_ao_b, w_gv_b, w_gg_b, w_out_b, gains[1], biases[1], l, alpha)
        x = _ffn_ln(x, ffn2_gate, ffn2_up, ffn2_down, gains[2], biases[2], l, alpha)

        hl = hl4.transpose(0, 2, 1, 3).reshape(batch, groups, 2 * SSM_STATE)
        hrp_l.append(hl[..., :SSM_STATE])
        hip_l.append(hl[..., SSM_STATE:])
        ks_l.append(k_s.reshape(dec_batch, dec_seq, N_HEADS, HEAD_DIM))
        vs_l.append(v_s.reshape(dec_batch, dec_seq, N_HEADS, HEAD_DIM))
        hn = hn_g.transpose(1, 0, 2)
        hrs_l.append(hn[..., :SSM_STATE])
        his_l.append(hn[..., SSM_STATE:])

    y_prompt = x[:t_p].reshape(batch, seq, d_model)
    y_sample = x[t_p:t_p + t_s].reshape(dec_batch, dec_seq, d_model)
    return (y_prompt, y_sample,
            k_all[:, :t_p].reshape(depth, batch, seq, N_HEADS, HEAD_DIM),
            v_all[:, :t_p].reshape(depth, batch, seq, N_HEADS, HEAD_DIM),
            jnp.stack(hrp_l), jnp.stack(hip_l),
            jnp.stack(ks_l), jnp.stack(vs_l), jnp.stack(hrs_l), jnp.stack(his_l))
```
